```python
import math
import jax, jax.numpy as jnp
from jax import lax
import numpy as np

D_MODEL = 1024
BATCH = 8
SEQ = 4096
DEPTH = 2

A_HEADS = 8
A_HEAD_DIM = 64
A_ROT_DIM = A_HEAD_DIM // 4
A_WIDTH = A_HEADS * A_HEAD_DIM
MOBA_BLOCK = 256
MOBA_TOPK = 3
MOBA_Q_CHUNK = 16
B_HEADS = 8
B_NOPE_DIM = 64
B_ROPE_DIM = 32
B_QK_DIM = B_NOPE_DIM + B_ROPE_DIM
B_V_DIM = 64
B_WIDTH = B_HEADS * B_V_DIM
B_Q_RANK = 384
B_KV_RANK = 256
ATTN_Q_BLOCK = 128
ROPE_THETA = 500000.0
N_BRANCH = 2
EPS = 1e-6
NEG = -1e30
D_FF = 2816
N_EXPERTS = 8
EXPERT_TOPK = 2
N_DENSE = (DEPTH + 1) // 2
N_MOE = DEPTH // 2

IN_SPLITS = (A_WIDTH, A_WIDTH, A_WIDTH, B_Q_RANK, B_KV_RANK, B_ROPE_DIM, N_BRANCH * D_MODEL)
IN_WIDTH = sum(IN_SPLITS)
SPLIT_IDX = tuple(int(v) for v in np.cumsum(IN_SPLITS)[:-1])

kernel_name = "hybrid_moba_mla_gated_moe_block"


def rms_norm(x, g):
    xf = x.astype(jnp.float32)
    y = xf * lax.rsqrt(jnp.mean(xf * xf, axis=-1, keepdims=True) + EPS)
    return (y * g.astype(jnp.float32)).astype(x.dtype)


def rope_tables(seq, rot_dim):
    inv = ROPE_THETA ** (-jnp.arange(0, rot_dim, 2, dtype=jnp.float32) / rot_dim)
    ang = jnp.arange(seq, dtype=jnp.float32)[:, None] * inv[None, :]
    return jnp.cos(ang), jnp.sin(ang)


def apply_rope(x, cos, sin):
    xf = x.astype(jnp.float32)
    x1, x2 = jnp.split(xf, 2, axis=-1)
    return jnp.concatenate([x1 * cos - x2 * sin, x2 * cos + x1 * sin], axis=-1).astype(x.dtype)


def rope_slice(x, start, rot_dim, cos, sin):
    return jnp.concatenate([x[..., :start],
                            apply_rope(x[..., start:start + rot_dim], cos, sin),
                            x[..., start + rot_dim:]], axis=-1)


def moba_attention(q, k, v):
    b, h, s, dh = q.shape
    nb = -(-s // MOBA_BLOCK)
    pad = nb * MOBA_BLOCK - s
    kp = jnp.pad(k, ((0, 0), (0, 0), (0, pad), (0, 0))).reshape(b, h, nb, MOBA_BLOCK, dh)
    vp = jnp.pad(v, ((0, 0), (0, 0), (0, pad), (0, 0))).reshape(b, h, nb, MOBA_BLOCK, dh)
    k_mean = jnp.mean(kp.astype(jnp.float32), axis=3)
    pos = jnp.arange(s)
    q_blk = pos // MOBA_BLOCK
    gate = jnp.einsum('bhsd,bhnd->bhsn', q.astype(jnp.float32), k_mean)
    past = jnp.arange(nb)[None, :] < q_blk[:, None]
    gate = jnp.where(past, gate, -jnp.inf)
    topk = min(MOBA_TOPK, nb)
    _, sel = lax.top_k(gate, topk)
    own = jnp.broadcast_to(q_blk.astype(sel.dtype), (b, h, s))[..., None]
    sel = jnp.concatenate([sel, own], axis=-1)
    n_slot = topk + 1
    r = jnp.arange(n_slot)
    slot_ok = (r[None, :] == topk) | (r[None, :] < q_blk[:, None])

    base = (jnp.arange(b)[:, None] * h + jnp.arange(h)[None, :]) * nb
    flat_idx = base[:, :, None, None].astype(sel.dtype) + sel
    k_flat = kp.reshape(b * h * nb, MOBA_BLOCK, dh)
    v_flat = vp.reshape(b * h * nb, MOBA_BLOCK, dh)

    nc = s // MOBA_Q_CHUNK
    c = MOBA_Q_CHUNK
    to_chunks = lambda t: t.reshape(b, h, nc, c, *t.shape[3:]).transpose(2, 0, 1, 3, *range(4, t.ndim + 1))
    q_c = to_chunks(q)
    fidx_c = to_chunks(flat_idx)
    sel_c = to_chunks(sel)
    ok_c = slot_ok.reshape(nc, c, n_slot)
    t_c = pos.reshape(nc, c)
    scale = 1.0 / math.sqrt(dh)
    p_off = jnp.arange(MOBA_BLOCK)

    def chunk_fn(args):
        qc, fidx, selc, okc, tc = args
        kg = jnp.take(k_flat, fidx, axis=0)
        vg = jnp.take(v_flat, fidx, axis=0)
        sc = jnp.einsum('bhcd,bhcrpd->bhcrp', qc, kg,
                        preferred_element_type=jnp.float32) * scale
        kpos = selc[..., None] * MOBA_BLOCK + p_off
        valid = okc[None, None, :, :, None] & (kpos <= tc[None, None, :, None, None])
        sc = jnp.where(valid, sc, NEG)
        pr = jax.nn.softmax(sc.reshape(b, h, c, -1), axis=-1).reshape(sc.shape)
        return jnp.einsum('bhcrp,bhcrpd->bhcd', pr.astype(vg.dtype), vg)

    out = lax.map(chunk_fn, (q_c, fidx_c, sel_c, ok_c, t_c))
    return out.transpose(1, 2, 0, 3, 4).reshape(b, h, s, dh)


def causal_attention(q, k, v):
    b, h, s, d = q.shape
    nq = s // ATTN_Q_BLOCK
    q_b = q.reshape(b, h, nq, ATTN_Q_BLOCK, d).transpose(2, 0, 1, 3, 4)
    kpos = jnp.arange(s)
    scale = 1.0 / math.sqrt(d)

    def blk(args):
        qc, i = args
        sc = jnp.einsum('bhqd,bhkd->bhqk', qc, k, preferred_element_type=jnp.float32) * scale
        qpos = i * ATTN_Q_BLOCK + jnp.arange(ATTN_Q_BLOCK)
        sc = jnp.where(kpos[None, :] <= qpos[:, None], sc, NEG)
        pr = jax.nn.softmax(sc, axis=-1)
        return jnp.einsum('bhqk,bhkd->bhqd', pr.astype(v.dtype), v)

    out = lax.map(blk, (q_b, jnp.arange(nq)))
    return out.transpose(1, 2, 0, 3, 4).reshape(b, h, s, -1)


def token_mixer(x, attn_norm, w_in, moba_qn, moba_kn, cq_norm, w_uq, ckv_norm, w_ukv,
                mla_qn, mla_kn, w_br_a, w_br_b, w_out):
    b, s, _ = x.shape
    hn = rms_norm(x, attn_norm)
    proj = hn @ w_in
    qa, ka, va, cq, ckv, kr, gates = jnp.split(proj, SPLIT_IDX, axis=-1)
    heads = lambda t, n: t.reshape(b, s, n, -1).transpose(0, 2, 1, 3)

    cos_a, sin_a = rope_tables(s, A_ROT_DIM)
    qa = rope_slice(rms_norm(heads(qa, A_HEADS), moba_qn), 0, A_ROT_DIM, cos_a, sin_a)
    ka = rope_slice(rms_norm(heads(ka, A_HEADS), moba_kn), 0, A_ROT_DIM, cos_a, sin_a)
    va = heads(va, A_HEADS)
    ya = moba_attention(qa, ka, va)
    ya = ya.transpose(0, 2, 1, 3).reshape(b, s, A_WIDTH)

    cos_b, sin_b = rope_tables(s, B_ROPE_DIM)
    qb = heads(rms_norm(cq, cq_norm) @ w_uq, B_HEADS)
    kv = heads(rms_norm(ckv, ckv_norm) @ w_ukv, B_HEADS)
    k_nope, vb = jnp.split(kv, [B_NOPE_DIM], axis=-1)
    k_rope = jnp.broadcast_to(kr[:, None], (b, B_HEADS, s, B_ROPE_DIM))
    kb = jnp.concatenate([k_nope, k_rope], axis=-1)
    qb = rope_slice(rms_norm(qb, mla_qn), B_NOPE_DIM, B_ROPE_DIM, cos_b, sin_b)
    kb = rope_slice(rms_norm(kb, mla_kn), B_NOPE_DIM, B_ROPE_DIM, cos_b, sin_b)
    yb = causal_attention(qb, kb, vb)
    yb = yb.transpose(0, 2, 1, 3).reshape(b, s, B_WIDTH)

    g_a, g_b = jnp.split(gates, 2, axis=-1)
    merged = jax.nn.sigmoid(g_a) * (ya @ w_br_a) + jax.nn.sigmoid(g_b) * (yb @ w_br_b)
    return merged @ w_out


def swiglu(h, w_gu, w_down):
    g, u = jnp.split(h @ w_gu, 2, axis=-1)
    return (jax.nn.silu(g) * u) @ w_down


def moe_swiglu(h, router_w, w_gu, w_down):
    logits = jnp.einsum('bsd,de->bse', h, router_w, preferred_element_type=jnp.float32)
    top_v, top_i = lax.top_k(logits, EXPERT_TOPK)
    probs = jax.nn.softmax(top_v, axis=-1)
    gate = jnp.einsum('bsk,bske->bse', probs,
                      jax.nn.one_hot(top_i, N_EXPERTS, dtype=jnp.float32))
    y = jnp.zeros_like(h)
    for e in range(N_EXPERTS):
        y = y + gate[..., e:e + 1].astype(h.dtype) * swiglu(h, w_gu[e], w_down[e])
    return y


def setup_inputs(seed: int = 0) -> dict:
    key = jax.random.key(seed)
    ks = jax.random.split(key, 24)
    f32 = jnp.float32

    def w(k, shape, fan_in):
        return jax.random.normal(k, shape, f32) * (fan_in ** -0.5)

    def g(k, shape):
        return 1.0 + 0.1 * jax.random.normal(k, shape, f32)

    return {
        "x": jax.random.normal(ks[0], (BATCH, SEQ, D_MODEL), f32),
        "attn_norm": g(ks[1], (DEPTH, D_MODEL)),
        "w_in": w(ks[2], (DEPTH, D_MODEL, IN_WIDTH), D_MODEL),
        "moba_q_norm": g(ks[3], (DEPTH, A_HEAD_DIM)),
        "moba_k_norm": g(ks[4], (DEPTH, A_HEAD_DIM)),
        "mla_cq_norm": g(ks[5], (DEPTH, B_Q_RANK)),
        "w_uq": w(ks[6], (DEPTH, B_Q_RANK, B_HEADS * B_QK_DIM), B_Q_RANK),
        "mla_ckv_norm": g(ks[7], (DEPTH, B_KV_RANK)),
        "w_ukv": w(ks[8], (DEPTH, B_KV_RANK, B_HEADS * (B_NOPE_DIM + B_V_DIM)), B_KV_RANK),
        "mla_q_norm": g(ks[9], (DEPTH, B_QK_DIM)),
        "mla_k_norm": g(ks[10], (DEPTH, B_QK_DIM)),
        "w_branch_a": w(ks[11], (DEPTH, A_WIDTH, D_MODEL), A_WIDTH),
        "w_branch_b": w(ks[12], (DEPTH, B_WIDTH, D_MODEL), B_WIDTH),
        "w_out": w(ks[13], (DEPTH, D_MODEL, D_MODEL), D_MODEL),
        "ffn_norm": g(ks[14], (DEPTH, D_MODEL)),
        "dense_w_gate_up": w(ks[15], (N_DENSE, D_MODEL, 2 * D_FF), D_MODEL),
        "dense_w_down": w(ks[16], (N_DENSE, D_FF, D_MODEL), D_FF),
        "router_w": w(ks[17], (N_MOE, D_MODEL, N_EXPERTS), D_MODEL),
        "expert_w_gate_up": w(ks[18], (N_MOE, N_EXPERTS, D_MODEL, 2 * D_FF), D_MODEL),
        "expert_w_down": w(ks[19], (N_MOE, N_EXPERTS, D_FF, D_MODEL), D_FF),
    }


def reference(x, attn_norm, w_in, moba_q_norm, moba_k_norm, mla_cq_norm, w_uq, mla_ckv_norm,
              w_ukv, mla_q_norm, mla_k_norm, w_branch_a, w_branch_b, w_out, ffn_norm,
              dense_w_gate_up, dense_w_down, router_w, expert_w_gate_up, expert_w_down):
    for l in range(DEPTH):
        x = x + token_mixer(x, attn_norm[l], w_in[l], moba_q_norm[l], moba_k_norm[l],
                            mla_cq_norm[l], w_uq[l], mla_ckv_norm[l], w_ukv[l],
                            mla_q_norm[l], mla_k_norm[l], w_branch_a[l], w_branch_b[l], w_out[l])
        h = rms_norm(x, ffn_norm[l])
        if l % 2 == 0:
            x = x + swiglu(h, dense_w_gate_up[l // 2], dense_w_down[l // 2])
        else:
            x = x + moe_swiglu(h, router_w[l // 2], expert_w_gate_up[l // 2], expert_w_down[l // 2])
    return x
```

```python
import functools
import math

import jax
import jax.numpy as jnp
import numpy as np
from jax import lax
from jax.experimental import pallas as pl
from jax.experimental.pallas import tpu as pltpu

F32 = jnp.float32
BF16 = jnp.bfloat16

D_MODEL = 1024
N_HEADS = 8
A_HEAD_DIM = 64
A_ROT_DIM = 16
MOBA_BLOCK = 256
MOBA_TOPK = 3
MAX_MOBA_BLOCKS = 16
B_NOPE_DIM = 64
B_ROPE_DIM = 32
B_QK_DIM = B_NOPE_DIM + B_ROPE_DIM
B_V_DIM = 64
B_Q_RANK = 384
B_KV_RANK = 256
ROPE_THETA = 500000.0
EPS = 1e-6
NEG = -1e30
M_INIT = -3e38
D_FF = 2816
N_EXPERTS = 8

LANES = 128
SLOT = LANES
MXU_DIM = 256
HW = N_HEADS * SLOT
BIAS_LANE0 = A_HEAD_DIM
V_WIDTH = N_HEADS * B_V_DIM

C_QA, C_KA, C_VA = 0, HW, 2 * HW
C_CQ = C_VA + V_WIDTH
C_CKV = C_CQ + B_Q_RANK
C_KR = C_CKV + B_KV_RANK
N_PROJ = C_KR + LANES

VMEM_LIMIT = 56 * 1024 * 1024


def _cparams(n_axes):
    return pltpu.CompilerParams(dimension_semantics=("arbitrary",) * n_axes,
                                vmem_limit_bytes=VMEM_LIMIT)


def _dot(a, b):
    return jnp.dot(a, b, preferred_element_type=F32)


def _dot_nt(a, b):
    return lax.dot_general(a, b, (((1,), (1,)), ((), ())), preferred_element_type=F32)


def _split_bf16(a):
    hi = a.astype(BF16)
    lo = (a - hi.astype(F32)).astype(BF16)
    return hi, lo


def _dot3(a, b):
    ah, al = _split_bf16(a)
    bh, bl = _split_bf16(b)
    return _dot(ah, bh) + (_dot(ah, bl) + _dot(al, bh))


def _dot3_nt(a, b):
    ah, al = _split_bf16(a)
    bh, bl = _split_bf16(b)
    return _dot_nt(ah, bh) + (_dot_nt(ah, bl) + _dot_nt(al, bh))


def _rms_rows(xf, g):
    ms = jnp.mean(xf * xf, axis=-1, keepdims=True)
    return xf * lax.rsqrt(ms + EPS) * g


def _sigmoid(z):
    return 1.0 / (1.0 + jnp.exp(-z))


def _slot_norm_rope(x, g_ref, seg_ref, inv_n, rope_ref, shift):
    cos = rope_ref[0]
    sin_fwd = rope_ref[1]
    sin_bwd = rope_ref[2]
    slots = []
    for c in range(HW // MXU_DIM):
        xc = x[:, c * MXU_DIM:(c + 1) * MXU_DIM]
        ss = _dot((xc * xc).astype(BF16), seg_ref[...])
        xn = xc * lax.rsqrt(ss * inv_n + EPS) * g_ref[:, c * MXU_DIM:(c + 1) * MXU_DIM]
        for j in range(MXU_DIM // SLOT):
            xs = xn[:, j * SLOT:(j + 1) * SLOT]
            slots.append(xs * cos + pltpu.roll(xs, shift, 1) * sin_fwd
                         + pltpu.roll(xs, SLOT - shift, 1) * sin_bwd)
    return slots


def _prologue_kernel(x_ref, an_ref, w_ref, wuq_ref, wukv_ref, cqn_ref, ckvn_ref,
                     qna_ref, kna_ref, qnb_ref, knb_ref, seg_ref, place_ref,
                     ropea_ref, ropeb_ref,
                     qa_ref, ka_ref, va_ref, qb_ref, kb_ref, vb_ref, kmean_scr):
    s = pl.program_id(1)
    tm = x_ref.shape[1]

    @pl.when(s == 0)
    def _():
        kmean_scr[...] = jnp.zeros_like(kmean_scr)

    hn = _rms_rows(x_ref[0], an_ref[...]).astype(BF16)
    proj = _dot(hn, w_ref[...])

    qa = _slot_norm_rope(proj[:, C_QA:C_QA + HW], qna_ref, seg_ref, 1.0 / A_HEAD_DIM,
                         ropea_ref, A_ROT_DIM // 2)
    ka = _slot_norm_rope(proj[:, C_KA:C_KA + HW], kna_ref, seg_ref, 1.0 / A_HEAD_DIM,
                         ropea_ref, A_ROT_DIM // 2)
    va_ref[0] = proj[:, C_VA:C_VA + V_WIDTH].astype(BF16)

    kmean = kmean_scr[...]
    gates = [_dot3_nt(kmean[:, h * SLOT:(h + 1) * SLOT], qa[h]) for h in range(N_HEADS)]
    g3 = jnp.stack(gates, axis=0)
    blk = lax.broadcasted_iota(jnp.int32, g3.shape, 1)
    g3 = jnp.where(blk < s, g3, -jnp.inf)
    rank3 = jnp.zeros(g3.shape, F32)
    for n in range(MAX_MOBA_BLOCKS):
        row = g3[:, n:n + 1, :]
        rank3 = rank3 + jnp.where((row > g3) | ((row == g3) & (blk > n)), 1.0, 0.0)
    sel3 = jnp.where(((rank3 < MOBA_TOPK) & (blk < s)) | (blk == s), 1.0, 0.0)
    sel_t = sel3.reshape(N_HEADS * MAX_MOBA_BLOCKS, tm)
    placed = _dot(sel_t.T.astype(BF16), place_ref[...])
    lane = lax.broadcasted_iota(jnp.int32, (tm, SLOT), 1)
    is_bias = (lane >= BIAS_LANE0) & (lane < BIAS_LANE0 + MAX_MOBA_BLOCKS)
    scale_a = 1.0 / math.sqrt(A_HEAD_DIM)
    for h in range(N_HEADS):
        bias = (placed[:, h * SLOT:(h + 1) * SLOT] - 1.0) * (-NEG)
        qa_ref[0, :, h * SLOT:(h + 1) * SLOT] = jnp.where(is_bias, bias, qa[h] * scale_a).astype(BF16)
        ka_ref[0, :, h * SLOT:(h + 1) * SLOT] = jnp.where(lane == BIAS_LANE0 + s, 1.0, ka[h]).astype(BF16)

    km = jnp.concatenate([jnp.mean(k, axis=0, keepdims=True) for k in ka], axis=1)
    kmean_scr[pl.ds(s, 1), :] = km

    cq = _rms_rows(proj[:, C_CQ:C_CQ + B_Q_RANK], cqn_ref[...]).astype(BF16)
    qb = _dot(cq, wuq_ref[...])
    ckv = _rms_rows(proj[:, C_CKV:C_CKV + B_KV_RANK], ckvn_ref[...]).astype(BF16)
    kv = _dot(ckv, wukv_ref[...])
    vb_ref[0] = kv[:, HW:HW + V_WIDTH].astype(BF16)
    kr = pltpu.roll(proj[:, C_KR:C_KR + LANES], B_NOPE_DIM, 1)
    kb = kv[:, :HW] + jnp.concatenate([kr] * N_HEADS, axis=1)
    scale_b = 1.0 / math.sqrt(B_QK_DIM)
    qb_s = _slot_norm_rope(qb, qnb_ref, seg_ref, 1.0 / B_QK_DIM, ropeb_ref, B_ROPE_DIM // 2)
    kb_s = _slot_norm_rope(kb, knb_ref, seg_ref, 1.0 / B_QK_DIM, ropeb_ref, B_ROPE_DIM // 2)
    for h in range(N_HEADS):
        qb_ref[0, :, h * SLOT:(h + 1) * SLOT] = (qb_s[h] * scale_b).astype(BF16)
        kb_ref[0, :, h * SLOT:(h + 1) * SLOT] = kb_s[h].astype(BF16)


def _prologue(x, p, rope_a, rope_b):
    b, s, _ = x.shape
    tm = MOBA_BLOCK
    assert s % tm == 0 and s // tm <= MAX_MOBA_BLOCKS
    full = lambda a: pl.BlockSpec(a.shape, lambda i, j: (0,) * a.ndim)
    tok = lambda w: pl.BlockSpec((1, tm, w), lambda i, j: (i, j, 0))
    rope = pl.BlockSpec((3, tm, LANES), lambda i, j: (0, j, 0))
    consts = [p["attn_norm"], p["w_proj"], p["w_uq"], p["w_ukv"], p["cq_norm"], p["ckv_norm"],
              p["qn_a"], p["kn_a"], p["qn_b"], p["kn_b"], p["seg"], p["place"]]
    out_shape = [jax.ShapeDtypeStruct((b, s, w), BF16) for w in (HW, HW, V_WIDTH, HW, HW, V_WIDTH)]
    return pl.pallas_call(
        _prologue_kernel,
        grid=(b, s // tm),
        in_specs=[tok(D_MODEL)] + [full(a) for a in consts] + [rope, rope],
        out_specs=[tok(w) for w in (HW, HW, V_WIDTH, HW, HW, V_WIDTH)],
        out_shape=out_shape,
        scratch_shapes=[pltpu.VMEM((MAX_MOBA_BLOCKS, HW), F32)],
        compiler_params=_cparams(2),
        name="prologue",
    )(x, *consts, rope_a, rope_b)


def _attn_kernel(q_ref, k_ref, v_ref, o_ref):
    i = pl.program_id(2)
    t = q_ref.shape[1]
    row = lax.broadcasted_iota(jnp.int32, (t, t), 0)
    col = lax.broadcasted_iota(jnp.int32, (t, t), 1)
    causal = col <= row
    outs = []
    for hh in range(2):
        q = q_ref[0, :, hh * SLOT:(hh + 1) * SLOT]

        def step(n, carry, diagonal, hh=hh, q=q):
            m, l, acc = carry
            off = pl.multiple_of(n * t, t)
            k = k_ref[0, pl.ds(off, t), hh * SLOT:(hh + 1) * SLOT]
            v = v_ref[0, pl.ds(off, t), :]
            sc = _dot_nt(q, k)
            if diagonal:
                sc = jnp.where(causal, sc, NEG)
            m_new = jnp.maximum(m, jnp.max(sc, axis=1, keepdims=True))
            alpha = jnp.exp(m - m_new)
            pr = jnp.exp(sc - m_new)
            l = alpha * l + jnp.sum(pr, axis=1, keepdims=True)
            acc = alpha * acc + _dot(pr.astype(BF16), v)
            return m_new, l, acc

        init = (jnp.full((t, 1), M_INIT, F32), jnp.zeros((t, 1), F32), jnp.zeros((t, LANES), F32))
        carry = lax.fori_loop(0, i, functools.partial(step, diagonal=False), init)
        _, l, acc = step(i, carry, True)
        outs.append(acc / l)
    lane = lax.broadcasted_iota(jnp.int32, (t, LANES), 1)
    o_ref[0] = jnp.where(lane < B_V_DIM, outs[0], outs[1]).astype(o_ref.dtype)


def _attention(q, k, v):
    b, s, _ = q.shape
    t = MOBA_BLOCK
    return pl.pallas_call(
        _attn_kernel,
        grid=(b, N_HEADS // 2, s // t),
        in_specs=[pl.BlockSpec((1, t, 2 * SLOT), lambda bi, p, i: (bi, i, p)),
                  pl.BlockSpec((1, s, 2 * SLOT), lambda bi, p, i: (bi, 0, p)),
                  pl.BlockSpec((1, s, LANES), lambda bi, p, i: (bi, 0, p))],
        out_specs=pl.BlockSpec((1, t, LANES), lambda bi, p, i: (bi, i, p)),
        out_shape=jax.ShapeDtypeStruct((b, s, V_WIDTH), BF16),
        compiler_params=_cparams(3),
        name="attention",
    )(q, k, v)


def _merge_kernel(x_ref, an_ref, ya_ref, yb_ref, wg_ref, wa_ref, wb_ref, wo_ref, o_ref):
    xf = x_ref[...]
    hn = _rms_rows(xf, an_ref[...]).astype(BF16)
    gates = _dot(hn, wg_ref[...])
    pa = _dot(ya_ref[...], wa_ref[...])
    pb = _dot(yb_ref[...], wb_ref[...])
    merged = _sigmoid(gates[:, :D_MODEL]) * pa + _sigmoid(gates[:, D_MODEL:]) * pb
    o_ref[...] = xf + _dot(merged.astype(BF16), wo_ref[...])


def _merge(x2, p, ya2, yb2, tm=512):
    t = x2.shape[0]
    full = lambda a: pl.BlockSpec(a.shape, lambda i: (0,) * a.ndim)
    tok = lambda w: pl.BlockSpec((tm, w), lambda i: (i, 0))
    consts = [p["w_gates"], p["w_br_a"], p["w_br_b"], p["w_out"]]
    return pl.pallas_call(
        _merge_kernel,
        grid=(t // tm,),
        in_specs=[tok(D_MODEL), full(p["attn_norm"]), tok(V_WIDTH), tok(V_WIDTH)] + [full(a) for a in consts],
        out_specs=tok(D_MODEL),
        out_shape=jax.ShapeDtypeStruct((t, D_MODEL), F32),
        compiler_params=_cparams(1),
        name="merge",
    )(x2, p["attn_norm"], ya2, yb2, *consts)


FF_CHUNK = MXU_DIM


def _ffn_kernel(x_ref, fn_ref, wgu_ref, wd_ref, o_ref, act_scr):
    xf = x_ref[...]
    h = _rms_rows(xf, fn_ref[...]).astype(BF16)
    for c in range(D_FF // FF_CHUNK):
        g = _dot(h, wgu_ref[:, c * FF_CHUNK:(c + 1) * FF_CHUNK])
        u = _dot(h, wgu_ref[:, D_FF + c * FF_CHUNK:D_FF + (c + 1) * FF_CHUNK])
        act_scr[:, c * FF_CHUNK:(c + 1) * FF_CHUNK] = (g * _sigmoid(g) * u).astype(BF16)
    o_ref[...] = xf + _dot(act_scr[...], wd_ref[...])


def _ffn_dense(x2, fn, w_gu, w_down, tm=512):
    t = x2.shape[0]
    full = lambda a: pl.BlockSpec(a.shape, lambda i: (0,) * a.ndim)
    tok = pl.BlockSpec((tm, D_MODEL), lambda i: (i, 0))
    return pl.pallas_call(
        _ffn_kernel,
        grid=(t // tm,),
        in_specs=[tok, full(fn), full(w_gu), full(w_down)],
        out_specs=tok,
        out_shape=jax.ShapeDtypeStruct((t, D_MODEL), F32),
        scratch_shapes=[pltpu.VMEM((tm, D_FF), BF16)],
        compiler_params=_cparams(1),
        name="ffn_dense",
    )(x2, fn, w_gu, w_down)


MOE_FF_SPLIT = 2
MOE_FF_CHUNK = D_FF // MOE_FF_SPLIT


def _moe_kernel(x_ref, fn_ref, rw_ref, wg_ref, wu_ref, wd_ref, o_ref, h_scr, gate_scr, acc_scr):
    e = pl.program_id(1)
    c = pl.program_id(2)
    tm = x_ref.shape[0]
    lane = lax.broadcasted_iota(jnp.int32, (tm, LANES), 1)

    @pl.when((e == 0) & (c == 0))
    def _():
        xf = x_ref[...]
        hf = _rms_rows(xf, fn_ref[...])
        h_scr[...] = hf.astype(BF16)
        acc_scr[...] = xf
        logits = jnp.where(lane < N_EXPERTS, _dot3(hf, rw_ref[...]), -jnp.inf)
        m1 = jnp.max(logits, axis=1, keepdims=True)
        i1 = jnp.min(jnp.where(logits == m1, lane, LANES), axis=1, keepdims=True)
        rest = jnp.where(lane == i1, -jnp.inf, logits)
        m2 = jnp.max(rest, axis=1, keepdims=True)
        i2 = jnp.min(jnp.where(rest == m2, lane, LANES), axis=1, keepdims=True)
        e2 = jnp.exp(m2 - m1)
        gate_scr[...] = jnp.where(lane == i1, 1.0 / (1.0 + e2), jnp.where(lane == i2, e2 / (1.0 + e2), 0.0))

    h = h_scr[...]
    g = _dot(h, wg_ref[0])
    u = _dot(h, wu_ref[0])
    w_e = jnp.sum(jnp.where(lane == e, gate_scr[...], 0.0), axis=1, keepdims=True)
    act = (g * _sigmoid(g) * u * w_e).astype(BF16)
    acc_scr[...] += _dot(act, wd_ref[0])

    @pl.when((e == N_EXPERTS - 1) & (c == MOE_FF_SPLIT - 1))
    def _():
        o_ref[...] = acc_scr[...]


def _ffn_moe(x2, fn, router_w, w_gu, w_down, tm=512):
    t = x2.shape[0]
    tok = pl.BlockSpec((tm, D_MODEL), lambda i, e, c: (i, 0))
    full = lambda a: pl.BlockSpec(a.shape, lambda i, e, c: (0,) * a.ndim)
    return pl.pallas_call(
        _moe_kernel,
        grid=(t // tm, N_EXPERTS, MOE_FF_SPLIT),
        in_specs=[tok, full(fn), full(router_w),
                  pl.BlockSpec((1, D_MODEL, MOE_FF_CHUNK), lambda i, e, c: (e, 0, c)),
                  pl.BlockSpec((1, D_MODEL, MOE_FF_CHUNK), lambda i, e, c: (e, 0, c + MOE_FF_SPLIT)),
                  pl.BlockSpec((1, MOE_FF_CHUNK, D_MODEL), lambda i, e, c: (e, c, 0))],
        out_specs=tok,
        out_shape=jax.ShapeDtypeStruct((t, D_MODEL), F32),
        scratch_shapes=[pltpu.VMEM((tm, D_MODEL), BF16), pltpu.VMEM((tm, LANES), F32),
                        pltpu.VMEM((tm, D_MODEL), F32)],
        compiler_params=_cparams(3),
        name="ffn_moe",
    )(x2, fn, router_w, w_gu, w_gu, w_down)


def _slot_cols(w, dims):
    k = w.shape[0]
    w = w.reshape(k, N_HEADS, dims)
    return jnp.pad(w, ((0, 0), (0, 0), (0, SLOT - dims))).reshape(k, HW)


def _slot_vec(g, dims):
    return jnp.tile(jnp.pad(g, (0, SLOT - dims)), N_HEADS).reshape(1, HW)


def _rope_table(seq, rot_dim, lane0):
    half = rot_dim // 2
    inv = ROPE_THETA ** (-jnp.arange(0, rot_dim, 2, dtype=F32) / rot_dim)
    ang = jnp.arange(seq, dtype=F32)[:, None] * inv[None, :]
    cos, sin = jnp.cos(ang), jnp.sin(ang)
    c = jnp.ones((seq, SLOT), F32).at[:, lane0:lane0 + half].set(cos).at[:, lane0 + half:lane0 + rot_dim].set(cos)
    s_fwd = jnp.zeros((seq, SLOT), F32).at[:, lane0 + half:lane0 + rot_dim].set(sin)
    s_bwd = jnp.zeros((seq, SLOT), F32).at[:, lane0:lane0 + half].set(-sin)
    return jnp.stack([c, s_fwd, s_bwd])


def _constants():
    r = np.arange(MXU_DIM)
    seg = (r[:, None] // SLOT == r[None, :] // SLOT).astype(np.float32)
    place = np.zeros((N_HEADS * MAX_MOBA_BLOCKS, HW), np.float32)
    for h in range(N_HEADS):
        for n in range(MAX_MOBA_BLOCKS):
            place[h * MAX_MOBA_BLOCKS + n, h * SLOT + BIAS_LANE0 + n] = 1.0
    return jnp.asarray(seg, BF16), jnp.asarray(place, BF16)


def _layer_params(l, attn_norm, w_in, moba_q_norm, moba_k_norm, mla_cq_norm, w_uq, mla_ckv_norm,
                  w_ukv, mla_q_norm, mla_k_norm, w_branch_a, w_branch_b, w_out):
    wi = w_in[l]
    aw = N_HEADS * A_HEAD_DIM
    o_cq = 3 * aw
    o_ckv = o_cq + B_Q_RANK
    o_kr = o_ckv + B_KV_RANK
    o_g = o_kr + B_ROPE_DIM
    w_proj = jnp.concatenate([
        _slot_cols(wi[:, 0:aw], A_HEAD_DIM), _slot_cols(wi[:, aw:2 * aw], A_HEAD_DIM), wi[:, 2 * aw:3 * aw],
        wi[:, o_cq:o_ckv], wi[:, o_ckv:o_kr], jnp.pad(wi[:, o_kr:o_g], ((0, 0), (0, LANES - B_ROPE_DIM)))], axis=1)
    ukv = w_ukv[l].reshape(B_KV_RANK, N_HEADS, B_NOPE_DIM + B_V_DIM)
    w_ukv_s = jnp.concatenate([_slot_cols(ukv[:, :, :B_NOPE_DIM].reshape(B_KV_RANK, -1), B_NOPE_DIM),
                               ukv[:, :, B_NOPE_DIM:].reshape(B_KV_RANK, -1)], axis=1)
    seg, place = _constants()
    return {
        "attn_norm": attn_norm[l].reshape(1, D_MODEL),
        "w_proj": w_proj.astype(BF16),
        "w_gates": wi[:, o_g:].astype(BF16),
        "w_uq": _slot_cols(w_uq[l], B_QK_DIM).astype(BF16),
        "w_ukv": w_ukv_s.astype(BF16),
        "cq_norm": mla_cq_norm[l].reshape(1, B_Q_RANK),
        "ckv_norm": mla_ckv_norm[l].reshape(1, B_KV_RANK),
        "qn_a": _slot_vec(moba_q_norm[l], A_HEAD_DIM),
        "kn_a": _slot_vec(moba_k_norm[l], A_HEAD_DIM),
        "qn_b": _slot_vec(mla_q_norm[l], B_QK_DIM),
        "kn_b": _slot_vec(mla_k_norm[l], B_QK_DIM),
        "seg": seg,
        "place": place,
        "w_br_a": w_branch_a[l].astype(BF16),
        "w_br_b": w_branch_b[l].astype(BF16),
        "w_out": w_out[l].astype(BF16),
    }


def kernel(x, attn_norm, w_in, moba_q_norm, moba_k_norm, mla_cq_norm, w_uq, mla_ckv_norm, w_ukv,
           mla_q_norm, mla_k_norm, w_branch_a, w_branch_b, w_out, ffn_norm, dense_w_gate_up,
           dense_w_down, router_w, expert_w_gate_up, expert_w_down):
    b, s, d = x.shape
    depth = attn_norm.shape[0]
    rope_a = _rope_table(s, A_ROT_DIM, 0)
    rope_b = _rope_table(s, B_ROPE_DIM, B_NOPE_DIM)
    for l in range(depth):
        p = _layer_params(l, attn_norm, w_in, moba_q_norm, moba_k_norm, mla_cq_norm, w_uq,
                          mla_ckv_norm, w_ukv, mla_q_norm, mla_k_norm, w_branch_a, w_branch_b, w_out)
        qa, ka, va, qb, kb, vb = _prologue(x, p, rope_a, rope_b)
        ya = _attention(qa, ka, va)
        yb = _attention(qb, kb, vb)
        x2 = _merge(x.reshape(b * s, d), p, ya.reshape(b * s, V_WIDTH), yb.reshape(b * s, V_WIDTH))
        fn = ffn_norm[l].reshape(1, D_MODEL)
        if l % 2 == 0:
            x2 = _ffn_dense(x2, fn, dense_w_gate_up[l // 2].astype(BF16), dense_w_down[l // 2].astype(BF16))
        else:
            rw = jnp.pad(router_w[l // 2], ((0, 0), (0, LANES - N_EXPERTS)))
            x2 = _ffn_moe(x2, fn, rw, expert_w_gate_up[l // 2].astype(BF16), expert_w_down[l // 2].astype(BF16))
        x = x2.reshape(b, s, d)
    return x
```

```python
import functools
import math

import jax
import jax.numpy as jnp
import numpy as np
from jax import lax
from jax.experimental import pallas as pl
from jax.experimental.pallas import tpu as pltpu

F32 = jnp.float32
BF16 = jnp.bfloat16

D_MODEL = 1024
N_HEADS = 8
A_HEAD_DIM = 64
A_ROT_DIM = 16
MOBA_BLOCK = 256
MOBA_TOPK = 3
MAX_MOBA_BLOCKS = 16
B_NOPE_DIM = 64
B_ROPE_DIM = 32
B_QK_DIM = B_NOPE_DIM + B_ROPE_DIM
B_V_DIM = 64
B_Q_RANK = 384
B_KV_RANK = 256
ROPE_THETA = 500000.0
EPS = 1e-6
NEG = -1e30
M_INIT = -3e38
D_FF = 2816
N_EXPERTS = 8

LANES = 128
SLOT = LANES
MXU_DIM = 256
HW = N_HEADS * SLOT
BIAS_LANE0 = A_HEAD_DIM
V_WIDTH = N_HEADS * B_V_DIM

C_QA, C_KA, C_VA = 0, HW, 2 * HW
C_CQ = C_VA + V_WIDTH
C_CKV = C_CQ + B_Q_RANK
C_KR = C_CKV + B_KV_RANK
N_PROJ = C_KR + LANES

VMEM_LIMIT = 56 * 1024 * 1024


def _cparams(n_axes):
    return pltpu.CompilerParams(dimension_semantics=("arbitrary",) * n_axes,
                                vmem_limit_bytes=VMEM_LIMIT)


def _dot(a, b):
    return jnp.dot(a, b, preferred_element_type=F32)


def _dot_nt(a, b):
    return lax.dot_general(a, b, (((1,), (1,)), ((), ())), preferred_element_type=F32)


def _split_bf16(a):
    hi = a.astype(BF16)
    lo = (a - hi.astype(F32)).astype(BF16)
    return hi, lo


def _dot3(a, b):
    ah, al = _split_bf16(a)
    bh, bl = _split_bf16(b)
    return _dot(ah, bh) + (_dot(ah, bl) + _dot(al, bh))


def _dot3_nt(a, b):
    ah, al = _split_bf16(a)
    bh, bl = _split_bf16(b)
    return _dot_nt(ah, bh) + (_dot_nt(ah, bl) + _dot_nt(al, bh))


def _rms_rows(xf, g):
    ms = jnp.mean(xf * xf, axis=-1, keepdims=True)
    return xf * lax.rsqrt(ms + EPS) * g


def _sigmoid(z):
    return 1.0 / (1.0 + jnp.exp(-z))


def _slot_norm_rope(x, g_ref, seg_ref, inv_n, rope_ref, shift):
    cos = rope_ref[0]
    sin_fwd = rope_ref[1]
    sin_bwd = rope_ref[2]
    slots = []
    for c in range(HW // MXU_DIM):
        xc = x[:, c * MXU_DIM:(c + 1) * MXU_DIM]
        ss = _dot((xc * xc).astype(BF16), seg_ref[...])
        xn = xc * lax.rsqrt(ss * inv_n + EPS) * g_ref[:, c * MXU_DIM:(c + 1) * MXU_DIM]
        for j in range(MXU_DIM // SLOT):
            xs = xn[:, j * SLOT:(j + 1) * SLOT]
            slots.append(xs * cos + pltpu.roll(xs, shift, 1) * sin_fwd
                         + pltpu.roll(xs, SLOT - shift, 1) * sin_bwd)
    return slots


def _prologue_kernel(x_ref, an_ref, w_ref, wuq_ref, wukv_ref, cqn_ref, ckvn_ref,
                     qna_ref, kna_ref, qnb_ref, knb_ref, seg_ref, place_ref,
                     ropea_ref, ropeb_ref,
                     qa_ref, ka_ref, va_ref, qb_ref, kb_ref, vb_ref, kmean_scr):
    s = pl.program_id(1)
    tm = x_ref.shape[1]

    @pl.when(s == 0)
    def _():
        kmean_scr[...] = jnp.zeros_like(kmean_scr)

    hn = _rms_rows(x_ref[0], an_ref[...]).astype(BF16)
    proj = _dot(hn, w_ref[...])

    qa = _slot_norm_rope(proj[:, C_QA:C_QA + HW], qna_ref, seg_ref, 1.0 / A_HEAD_DIM,
                         ropea_ref, A_ROT_DIM // 2)
    ka = _slot_norm_rope(proj[:, C_KA:C_KA + HW], kna_ref, seg_ref, 1.0 / A_HEAD_DIM,
                         ropea_ref, A_ROT_DIM // 2)
    va_ref[0] = proj[:, C_VA:C_VA + V_WIDTH].astype(BF16)

    kmean = kmean_scr[...]
    gates = [_dot3_nt(kmean[:, h * SLOT:(h + 1) * SLOT], qa[h]) for h in range(N_HEADS)]
    g3 = jnp.stack(gates, axis=0)
    blk = lax.broadcasted_iota(jnp.int32, g3.shape, 1)
    g3 = jnp.where(blk < s, g3, -jnp.inf)
    rank3 = jnp.zeros(g3.shape, F32)
    for n in range(MAX_MOBA_BLOCKS):
        row = g3[:, n:n + 1, :]
        rank3 = rank3 + jnp.where((row > g3) | ((row == g3) & (blk > n)), 1.0, 0.0)
    sel3 = jnp.where(((rank3 < MOBA_TOPK) & (blk < s)) | (blk == s), 1.0, 0.0)
    sel_t = sel3.reshape(N_HEADS * MAX_MOBA_BLOCKS, tm)
    placed = _dot(sel_t.T.astype(BF16), place_ref[...])
    lane = lax.broadcasted_iota(jnp.int32, (tm, SLOT), 1)
    is_bias = (lane >= BIAS_LANE0) & (lane < BIAS_LANE0 + MAX_MOBA_BLOCKS)
    scale_a = 1.0 / math.sqrt(A_HEAD_DIM)
    for h in range(N_HEADS):
        bias = (placed[:, h * SLOT:(h + 1) * SLOT] - 1.0) * (-NEG)
        qa_ref[0, :, h * SLOT:(h + 1) * SLOT] = jnp.where(is_bias, bias, qa[h] * scale_a).astype(BF16)
        ka_ref[0, :, h * SLOT:(h + 1) * SLOT] = jnp.where(lane == BIAS_LANE0 + s, 1.0, ka[h]).astype(BF16)

    km = jnp.concatenate([jnp.mean(k, axis=0, keepdims=True) for k in ka], axis=1)
    kmean_scr[pl.ds(s, 1), :] = km

    cq = _rms_rows(proj[:, C_CQ:C_CQ + B_Q_RANK], cqn_ref[...]).astype(BF16)
    qb = _dot(cq, wuq_ref[...])
    ckv = _rms_rows(proj[:, C_CKV:C_CKV + B_KV_RANK], ckvn_ref[...]).astype(BF16)
    kv = _dot(ckv, wukv_ref[...])
    vb_ref[0] = kv[:, HW:HW + V_WIDTH].astype(BF16)
    kr = pltpu.roll(proj[:, C_KR:C_KR + LANES], B_NOPE_DIM, 1)
    kb = kv[:, :HW] + jnp.concatenate([kr] * N_HEADS, axis=1)
    scale_b = 1.0 / math.sqrt(B_QK_DIM)
    qb_s = _slot_norm_rope(qb, qnb_ref, seg_ref, 1.0 / B_QK_DIM, ropeb_ref, B_ROPE_DIM // 2)
    kb_s = _slot_norm_rope(kb, knb_ref, seg_ref, 1.0 / B_QK_DIM, ropeb_ref, B_ROPE_DIM // 2)
    for h in range(N_HEADS):
        qb_ref[0, :, h * SLOT:(h + 1) * SLOT] = (qb_s[h] * scale_b).astype(BF16)
        kb_ref[0, :, h * SLOT:(h + 1) * SLOT] = kb_s[h].astype(BF16)


def _prologue(x, p, rope_a, rope_b):
    b, s, _ = x.shape
    tm = MOBA_BLOCK
    assert s % tm == 0 and s // tm <= MAX_MOBA_BLOCKS
    full = lambda a: pl.BlockSpec(a.shape, lambda i, j: (0,) * a.ndim)
    tok = lambda w: pl.BlockSpec((1, tm, w), lambda i, j: (i, j, 0))
    rope = pl.BlockSpec((3, tm, LANES), lambda i, j: (0, j, 0))
    consts = [p["attn_norm"], p["w_proj"], p["w_uq"], p["w_ukv"], p["cq_norm"], p["ckv_norm"],
              p["qn_a"], p["kn_a"], p["qn_b"], p["kn_b"], p["seg"], p["place"]]
    out_shape = [jax.ShapeDtypeStruct((b, s, w), BF16) for w in (HW, HW, V_WIDTH, HW, HW, V_WIDTH)]
    return pl.pallas_call(
        _prologue_kernel,
        grid=(b, s // tm),
        in_specs=[tok(D_MODEL)] + [full(a) for a in consts] + [rope, rope],
        out_specs=[tok(w) for w in (HW, HW, V_WIDTH, HW, HW, V_WIDTH)],
        out_shape=out_shape,
        scratch_shapes=[pltpu.VMEM((MAX_MOBA_BLOCKS, HW), F32)],
        compiler_params=_cparams(2),
        name="prologue",
    )(x, *consts, rope_a, rope_b)


ATTN_TILE = 512


def _attn_kernel(q_ref, k_ref, v_ref, o_ref):
    i = pl.program_id(2)
    t = q_ref.shape[1]
    row = lax.broadcasted_iota(jnp.int32, (t, t), 0)
    col = lax.broadcasted_iota(jnp.int32, (t, t), 1)
    causal = col <= row
    qs = [q_ref[0, :, hh * SLOT:(hh + 1) * SLOT] for hh in range(2)]

    def step(n, carry, diagonal):
        off = pl.multiple_of(n * t, t)
        v = v_ref[0, pl.ds(off, t), :]
        new = []
        for hh in range(2):
            m, l, acc = carry[hh]
            k = k_ref[0, pl.ds(off, t), hh * SLOT:(hh + 1) * SLOT]
            sc = _dot_nt(qs[hh], k)
            if diagonal:
                sc = jnp.where(causal, sc, NEG)
            m_new = jnp.maximum(m, jnp.max(sc, axis=1, keepdims=True))
            alpha = jnp.exp(m - m_new)
            pr = jnp.exp(sc - m_new)
            l = alpha * l + jnp.sum(pr, axis=1, keepdims=True)
            acc = alpha * acc + _dot(pr.astype(BF16), v)
            new.append((m_new, l, acc))
        return tuple(new)

    init = (jnp.full((t, 1), M_INIT, F32), jnp.zeros((t, 1), F32), jnp.zeros((t, LANES), F32))
    carry = lax.fori_loop(0, i, functools.partial(step, diagonal=False), (init, init))
    (_, l0, acc0), (_, l1, acc1) = step(i, carry, True)
    lane = lax.broadcasted_iota(jnp.int32, (t, LANES), 1)
    o_ref[0] = jnp.where(lane < B_V_DIM, acc0 / l0, acc1 / l1).astype(o_ref.dtype)


def _attention(q, k, v):
    b, s, _ = q.shape
    t = min(ATTN_TILE, s)
    assert s % t == 0 and t % MOBA_BLOCK == 0
    return pl.pallas_call(
        _attn_kernel,
        grid=(b, N_HEADS // 2, s // t),
        in_specs=[pl.BlockSpec((1, t, 2 * SLOT), lambda bi, p, i: (bi, i, p)),
                  pl.BlockSpec((1, s, 2 * SLOT), lambda bi, p, i: (bi, 0, p)),
                  pl.BlockSpec((1, s, LANES), lambda bi, p, i: (bi, 0, p))],
        out_specs=pl.BlockSpec((1, t, LANES), lambda bi, p, i: (bi, i, p)),
        out_shape=jax.ShapeDtypeStruct((b, s, V_WIDTH), BF16),
        compiler_params=_cparams(3),
        name="attention",
    )(q, k, v)


def _merge_kernel(x_ref, an_ref, ya_ref, yb_ref, wg_ref, wa_ref, wb_ref, wo_ref, o_ref):
    xf = x_ref[...]
    hn = _rms_rows(xf, an_ref[...]).astype(BF16)
    gates = _dot(hn, wg_ref[...])
    pa = _dot(ya_ref[...], wa_ref[...])
    pb = _dot(yb_ref[...], wb_ref[...])
    merged = _sigmoid(gates[:, :D_MODEL]) * pa + _sigmoid(gates[:, D_MODEL:]) * pb
    o_ref[...] = xf + _dot(merged.astype(BF16), wo_ref[...])


def _merge(x2, p, ya2, yb2, tm=512):
    t = x2.shape[0]
    full = lambda a: pl.BlockSpec(a.shape, lambda i: (0,) * a.ndim)
    tok = lambda w: pl.BlockSpec((tm, w), lambda i: (i, 0))
    consts = [p["w_gates"], p["w_br_a"], p["w_br_b"], p["w_out"]]
    return pl.pallas_call(
        _merge_kernel,
        grid=(t // tm,),
        in_specs=[tok(D_MODEL), full(p["attn_norm"]), tok(V_WIDTH), tok(V_WIDTH)] + [full(a) for a in consts],
        out_specs=tok(D_MODEL),
        out_shape=jax.ShapeDtypeStruct((t, D_MODEL), F32),
        compiler_params=_cparams(1),
        name="merge",
    )(x2, p["attn_norm"], ya2, yb2, *consts)


FF_CHUNK = MXU_DIM


def _ffn_kernel(x_ref, fn_ref, wgu_ref, wd_ref, o_ref, act_scr):
    xf = x_ref[...]
    h = _rms_rows(xf, fn_ref[...]).astype(BF16)
    for c in range(D_FF // FF_CHUNK):
        g = _dot(h, wgu_ref[:, c * FF_CHUNK:(c + 1) * FF_CHUNK])
        u = _dot(h, wgu_ref[:, D_FF + c * FF_CHUNK:D_FF + (c + 1) * FF_CHUNK])
        act_scr[:, c * FF_CHUNK:(c + 1) * FF_CHUNK] = (g * _sigmoid(g) * u).astype(BF16)
    o_ref[...] = xf + _dot(act_scr[...], wd_ref[...])


def _ffn_dense(x2, fn, w_gu, w_down, tm=512):
    t = x2.shape[0]
    full = lambda a: pl.BlockSpec(a.shape, lambda i: (0,) * a.ndim)
    tok = pl.BlockSpec((tm, D_MODEL), lambda i: (i, 0))
    return pl.pallas_call(
        _ffn_kernel,
        grid=(t // tm,),
        in_specs=[tok, full(fn), full(w_gu), full(w_down)],
        out_specs=tok,
        out_shape=jax.ShapeDtypeStruct((t, D_MODEL), F32),
        scratch_shapes=[pltpu.VMEM((tm, D_FF), BF16)],
        compiler_params=_cparams(1),
        name="ffn_dense",
    )(x2, fn, w_gu, w_down)


MOE_TILE = 512
ROUTE_TILE = 512
DMA_TOKENS = 256
PACKED = D_MODEL // 2
R_I1, R_I2, R_RANK1, R_RANK2, R_P1, R_P2 = range(6)


def _pack_bf16_pairs(xf):
    bits = lax.bitcast_convert_type(xf.astype(BF16).astype(F32), jnp.uint32)
    half = xf.shape[1] // 2
    return bits[:, :half] | (bits[:, half:] >> 16)


def _unpack_bf16_pairs(w):
    hi = lax.bitcast_convert_type(w & jnp.uint32(0xFFFF0000), F32)
    lo = lax.bitcast_convert_type(w << 16, F32)
    return jnp.concatenate([hi, lo], axis=1)


def _router_kernel(x_ref, fn_ref, rw_ref, hp_ref, route_ref, cnt_ref, carry_scr):
    tm = x_ref.shape[0]

    @pl.when(pl.program_id(0) == 0)
    def _():
        carry_scr[...] = jnp.zeros_like(carry_scr)

    hf = _rms_rows(x_ref[...], fn_ref[...])
    hp_ref[...] = _pack_bf16_pairs(hf)
    lane = lax.broadcasted_iota(jnp.int32, (tm, LANES), 1)
    logits = jnp.where(lane < N_EXPERTS, _dot3(hf, rw_ref[...]), -jnp.inf)
    m1 = jnp.max(logits, axis=1, keepdims=True)
    i1 = jnp.min(jnp.where(logits == m1, lane, LANES), axis=1, keepdims=True)
    rest = jnp.where(lane == i1, -jnp.inf, logits)
    m2 = jnp.max(rest, axis=1, keepdims=True)
    i2 = jnp.min(jnp.where(rest == m2, lane, LANES), axis=1, keepdims=True)
    e2 = jnp.exp(m2 - m1)
    p1 = 1.0 / (1.0 + e2)
    p2 = e2 / (1.0 + e2)
    onehot = jnp.where((lane == i1) | (lane == i2), 1.0, 0.0)
    r = lax.broadcasted_iota(jnp.int32, (tm, tm), 0)
    c = lax.broadcasted_iota(jnp.int32, (tm, tm), 1)
    earlier = jnp.where(c < r, 1.0, 0.0).astype(BF16)
    before = _dot(earlier, onehot.astype(BF16)) + carry_scr[0:1, :]
    rank1 = jnp.sum(jnp.where(lane == i1, before, 0.0), axis=1, keepdims=True)
    rank2 = jnp.sum(jnp.where(lane == i2, before, 0.0), axis=1, keepdims=True)
    rec = jnp.zeros((tm, LANES), F32)
    for pos, val in ((R_I1, i1.astype(F32)), (R_I2, i2.astype(F32)), (R_RANK1, rank1), (R_RANK2, rank2),
                     (R_P1, p1), (R_P2, p2)):
        rec = jnp.where(lane == pos, val, rec)
    route_ref[...] = rec
    total = carry_scr[0:1, :] + jnp.sum(onehot, axis=0, keepdims=True)
    carry_scr[...] = jnp.broadcast_to(total, carry_scr.shape)
    cnt_ref[...] = jnp.broadcast_to(total, cnt_ref.shape)


def _router(x2, fn, router_w):
    t = x2.shape[0]
    tm = ROUTE_TILE
    full = lambda a: pl.BlockSpec(a.shape, lambda i: (0,) * a.ndim)
    return pl.pallas_call(
        _router_kernel,
        grid=(t // tm,),
        in_specs=[pl.BlockSpec((tm, D_MODEL), lambda i: (i, 0)), full(fn), full(router_w)],
        out_specs=[pl.BlockSpec((tm, PACKED), lambda i: (i, 0)), pl.BlockSpec((tm, LANES), lambda i: (i, 0)),
                   pl.BlockSpec((8, LANES), lambda i: (0, 0))],
        out_shape=[jax.ShapeDtypeStruct((t, PACKED), jnp.uint32), jax.ShapeDtypeStruct((t, LANES), F32),
                   jax.ShapeDtypeStruct((8, LANES), F32)],
        scratch_shapes=[pltpu.VMEM((8, LANES), F32)],
        compiler_params=_cparams(1),
        name="moe_router",
    )(x2, fn, router_w)


def _row_copy(src_ref, src_row, dst_ref, dst_row, sem):
    return pltpu.make_async_copy(src_ref.at[pl.ds(src_row, 1)], dst_ref.at[pl.ds(dst_row, 1)], sem)


def _scatter_kernel(dest_ref, hp_ref, xs_in_ref, xs_ref, sem):
    del xs_in_ref
    c = pl.program_id(0)
    n = dest_ref.shape[2]
    slot = c % 2

    def issue(t, carry):
        for k in range(2):
            _row_copy(hp_ref, c * n + t, xs_ref, dest_ref[0, k, t], sem.at[slot]).start()
        return carry

    lax.fori_loop(0, n, issue, 0)

    def drain(s):
        def one(t, carry):
            _row_copy(hp_ref, 0, xs_ref, 0, sem.at[s]).wait()
            return carry
        lax.fori_loop(0, 2 * n, one, 0)

    @pl.when(c > 0)
    def _():
        drain(1 - slot)

    @pl.when(c == pl.num_programs(0) - 1)
    def _():
        drain(slot)


def _scatter_rows(dest3, hp, n_rows):
    nc, _, n = dest3.shape
    xs0 = jnp.zeros((n_rows, PACKED), jnp.uint32)
    return pl.pallas_call(
        _scatter_kernel,
        grid=(nc,),
        in_specs=[pl.BlockSpec((1, 2, n), lambda c: (c, 0, 0), memory_space=pltpu.SMEM),
                  pl.BlockSpec(memory_space=pl.ANY), pl.BlockSpec(memory_space=pl.ANY)],
        out_specs=pl.BlockSpec(memory_space=pl.ANY),
        out_shape=jax.ShapeDtypeStruct((n_rows, PACKED), jnp.uint32),
        scratch_shapes=[pltpu.SemaphoreType.DMA((2,))],
        input_output_aliases={2: 0},
        compiler_params=_cparams(1),
        name="moe_scatter",
    )(dest3, hp, xs0)


def _experts_kernel(te_ref, tv_ref, xs_ref, wgu_ref, wd_ref, y_ref, act_scr):
    del te_ref
    j = pl.program_id(0)

    @pl.when(tv_ref[j] != 0)
    def _():
        h = _unpack_bf16_pairs(xs_ref[...]).astype(BF16)
        for c in range(D_FF // FF_CHUNK):
            g = _dot(h, wgu_ref[0, :, c * FF_CHUNK:(c + 1) * FF_CHUNK])
            u = _dot(h, wgu_ref[0, :, D_FF + c * FF_CHUNK:D_FF + (c + 1) * FF_CHUNK])
            act_scr[:, c * FF_CHUNK:(c + 1) * FF_CHUNK] = (g * _sigmoid(g) * u).astype(BF16)
        y_ref[...] = _pack_bf16_pairs(_dot(act_scr[...], wd_ref[0]))

    @pl.when(tv_ref[j] == 0)
    def _():
        y_ref[...] = jnp.zeros_like(y_ref)


def _experts(tile_expert, tile_valid, xs, w_gu, w_down):
    n_rows = xs.shape[0]
    tm = MOE_TILE
    row = pl.BlockSpec((tm, PACKED), lambda j, te, tv: (j, 0))
    return pl.pallas_call(
        _experts_kernel,
        grid_spec=pltpu.PrefetchScalarGridSpec(
            num_scalar_prefetch=2,
            grid=(n_rows // tm,),
            in_specs=[row,
                      pl.BlockSpec((1, D_MODEL, 2 * D_FF), lambda j, te, tv: (te[j], 0, 0)),
                      pl.BlockSpec((1, D_FF, D_MODEL), lambda j, te, tv: (te[j], 0, 0))],
            out_specs=row,
            scratch_shapes=[pltpu.VMEM((tm, D_FF), BF16)]),
        out_shape=jax.ShapeDtypeStruct((n_rows, PACKED), jnp.uint32),
        compiler_params=_cparams(1),
        name="moe_experts",
    )(tile_expert, tile_valid, xs, w_gu, w_down)


def _combine_kernel(dcur_ref, dnext_ref, x_ref, route_ref, y_ref, o_ref, ybuf, sem):
    c = pl.program_id(0)
    last = pl.num_programs(0) - 1
    n = dcur_ref.shape[2]
    slot = c % 2

    def issue(dref, s):
        def body(t, carry):
            for k in range(2):
                _row_copy(y_ref, dref[0, k, t], ybuf.at[s, k], t, sem.at[s]).start()
            return carry
        lax.fori_loop(0, n, body, 0)

    @pl.when(c == 0)
    def _():
        issue(dcur_ref, 0)

    @pl.when(c < last)
    def _():
        issue(dnext_ref, 1 - slot)

    def one(t, carry):
        _row_copy(y_ref, 0, ybuf.at[slot, 0], 0, sem.at[slot]).wait()
        return carry
    lax.fori_loop(0, 2 * n, one, 0)

    rec = route_ref[...]
    y1 = _unpack_bf16_pairs(ybuf[slot, 0])
    y2 = _unpack_bf16_pairs(ybuf[slot, 1])
    o_ref[...] = x_ref[...] + (rec[:, R_P1:R_P1 + 1] * y1 + rec[:, R_P2:R_P2 + 1] * y2)


def _combine(dest3, x2, route, y):
    t = x2.shape[0]
    nc, _, n = dest3.shape
    tok = lambda w: pl.BlockSpec((n, w), lambda c: (c, 0))
    return pl.pallas_call(
        _combine_kernel,
        grid=(nc,),
        in_specs=[pl.BlockSpec((1, 2, n), lambda c: (c, 0, 0), memory_space=pltpu.SMEM),
                  pl.BlockSpec((1, 2, n), lambda c: (jnp.minimum(c + 1, nc - 1), 0, 0), memory_space=pltpu.SMEM),
                  tok(D_MODEL), tok(LANES), pl.BlockSpec(memory_space=pl.ANY)],
        out_specs=tok(D_MODEL),
        out_shape=jax.ShapeDtypeStruct((t, D_MODEL), F32),
        scratch_shapes=[pltpu.VMEM((2, 2, n, PACKED), jnp.uint32), pltpu.SemaphoreType.DMA((2,))],
        compiler_params=_cparams(1),
        name="moe_combine",
    )(dest3, dest3, x2, route, y)


def _ffn_moe(x2, fn, router_w, w_gu, w_down):
    t = x2.shape[0]
    hp, route, cnt = _router(x2, fn, router_w)
    i1 = route[:, R_I1].astype(jnp.int32)
    i2 = route[:, R_I2].astype(jnp.int32)
    counts = cnt[0, :N_EXPERTS].astype(jnp.int32)
    padded = (counts + MOE_TILE - 1) // MOE_TILE * MOE_TILE
    ends = jnp.cumsum(padded)
    offs = ends - padded
    experts = jnp.arange(N_EXPERTS, dtype=jnp.int32)
    off_of = lambda i: jnp.sum(jnp.where(i[:, None] == experts[None, :], offs[None, :], 0), axis=1)
    dest = jnp.stack([off_of(i1) + route[:, R_RANK1].astype(jnp.int32),
                      off_of(i2) + route[:, R_RANK2].astype(jnp.int32)])
    dest3 = dest.reshape(2, t // DMA_TOKENS, DMA_TOKENS).transpose(1, 0, 2)
    n_rows = 2 * t + N_EXPERTS * MOE_TILE
    tile_start = jnp.arange(n_rows // MOE_TILE, dtype=jnp.int32) * MOE_TILE
    tile_expert = jnp.minimum(jnp.sum((tile_start[:, None] >= ends[None, :]).astype(jnp.int32), axis=1),
                              N_EXPERTS - 1)
    tile_valid = (tile_start < ends[-1]).astype(jnp.int32)
    xs = _scatter_rows(dest3, hp, n_rows)
    y = _experts(tile_expert, tile_valid, xs, w_gu, w_down)
    return _combine(dest3, x2, route, y)


def _slot_cols(w, dims):
    k = w.shape[0]
    w = w.reshape(k, N_HEADS, dims)
    return jnp.pad(w, ((0, 0), (0, 0), (0, SLOT - dims))).reshape(k, HW)


def _slot_vec(g, dims):
    return jnp.tile(jnp.pad(g, (0, SLOT - dims)), N_HEADS).reshape(1, HW)


def _rope_table(seq, rot_dim, lane0):
    half = rot_dim // 2
    inv = ROPE_THETA ** (-jnp.arange(0, rot_dim, 2, dtype=F32) / rot_dim)
    ang = jnp.arange(seq, dtype=F32)[:, None] * inv[None, :]
    cos, sin = jnp.cos(ang), jnp.sin(ang)
    c = jnp.ones((seq, SLOT), F32).at[:, lane0:lane0 + half].set(cos).at[:, lane0 + half:lane0 + rot_dim].set(cos)
    s_fwd = jnp.zeros((seq, SLOT), F32).at[:, lane0 + half:lane0 + rot_dim].set(sin)
    s_bwd = jnp.zeros((seq, SLOT), F32).at[:, lane0:lane0 + half].set(-sin)
    return jnp.stack([c, s_fwd, s_bwd])


def _constants():
    r = np.arange(MXU_DIM)
    seg = (r[:, None] // SLOT == r[None, :] // SLOT).astype(np.float32)
    place = np.zeros((N_HEADS * MAX_MOBA_BLOCKS, HW), np.float32)
    for h in range(N_HEADS):
        for n in range(MAX_MOBA_BLOCKS):
            place[h * MAX_MOBA_BLOCKS + n, h * SLOT + BIAS_LANE0 + n] = 1.0
    return jnp.asarray(seg, BF16), jnp.asarray(place, BF16)


def _layer_params(l, attn_norm, w_in, moba_q_norm, moba_k_norm, mla_cq_norm, w_uq, mla_ckv_norm,
                  w_ukv, mla_q_norm, mla_k_norm, w_branch_a, w_branch_b, w_out):
    wi = w_in[l]
    aw = N_HEADS * A_HEAD_DIM
    o_cq = 3 * aw
    o_ckv = o_cq + B_Q_RANK
    o_kr = o_ckv + B_KV_RANK
    o_g = o_kr + B_ROPE_DIM
    w_proj = jnp.concatenate([
        _slot_cols(wi[:, 0:aw], A_HEAD_DIM), _slot_cols(wi[:, aw:2 * aw], A_HEAD_DIM), wi[:, 2 * aw:3 * aw],
        wi[:, o_cq:o_ckv], wi[:, o_ckv:o_kr], jnp.pad(wi[:, o_kr:o_g], ((0, 0), (0, LANES - B_ROPE_DIM)))], axis=1)
    ukv = w_ukv[l].reshape(B_KV_RANK, N_HEADS, B_NOPE_DIM + B_V_DIM)
    w_ukv_s = jnp.concatenate([_slot_cols(ukv[:, :, :B_NOPE_DIM].reshape(B_KV_RANK, -1), B_NOPE_DIM),
                               ukv[:, :, B_NOPE_DIM:].reshape(B_KV_RANK, -1)], axis=1)
    seg, place = _constants()
    return {
        "attn_norm": attn_norm[l].reshape(1, D_MODEL),
        "w_proj": w_proj.astype(BF16),
        "w_gates": wi[:, o_g:].astype(BF16),
        "w_uq": _slot_cols(w_uq[l], B_QK_DIM).astype(BF16),
        "w_ukv": w_ukv_s.astype(BF16),
        "cq_norm": mla_cq_norm[l].reshape(1, B_Q_RANK),
        "ckv_norm": mla_ckv_norm[l].reshape(1, B_KV_RANK),
        "qn_a": _slot_vec(moba_q_norm[l], A_HEAD_DIM),
        "kn_a": _slot_vec(moba_k_norm[l], A_HEAD_DIM),
        "qn_b": _slot_vec(mla_q_norm[l], B_QK_DIM),
        "kn_b": _slot_vec(mla_k_norm[l], B_QK_DIM),
        "seg": seg,
        "place": place,
        "w_br_a": w_branch_a[l].astype(BF16),
        "w_br_b": w_branch_b[l].astype(BF16),
        "w_out": w_out[l].astype(BF16),
    }


def kernel(x, attn_norm, w_in, moba_q_norm, moba_k_norm, mla_cq_norm, w_uq, mla_ckv_norm, w_ukv,
           mla_q_norm, mla_k_norm, w_branch_a, w_branch_b, w_out, ffn_norm, dense_w_gate_up,
           dense_w_down, router_w, expert_w_gate_up, expert_w_down):
    b, s, d = x.shape
    depth = attn_norm.shape[0]
    rope_a = _rope_table(s, A_ROT_DIM, 0)
    rope_b = _rope_table(s, B_ROPE_DIM, B_NOPE_DIM)
    for l in range(depth):
        p = _layer_params(l, attn_norm, w_in, moba_q_norm, moba_k_norm, mla_cq_norm, w_uq,
                          mla_ckv_norm, w_ukv, mla_q_norm, mla_k_norm, w_branch_a, w_branch_b, w_out)
        qa, ka, va, qb, kb, vb = _prologue(x, p, rope_a, rope_b)
        ya = _attention(qa, ka, va)
        yb = _attention(qb, kb, vb)
        x2 = _merge(x.reshape(b * s, d), p, ya.reshape(b * s, V_WIDTH), yb.reshape(b * s, V_WIDTH))
        fn = ffn_norm[l].reshape(1, D_MODEL)
        if l % 2 == 0:
            x2 = _ffn_dense(x2, fn, dense_w_gate_up[l // 2].astype(BF16), dense_w_down[l // 2].astype(BF16))
        else:
            rw = jnp.pad(router_w[l // 2], ((0, 0), (0, LANES - N_EXPERTS)))
            x2 = _ffn_moe(x2, fn, rw, expert_w_gate_up[l // 2].astype(BF16), expert_w_down[l // 2].astype(BF16))
        x = x2.reshape(b, s, d)
    return x
```

```python
import math

import jax
import jax.numpy as jnp
import numpy as np
from jax import lax
from jax.experimental import pallas as pl
from jax.experimental.pallas import tpu as pltpu

F32 = jnp.float32
BF16 = jnp.bfloat16

D_MODEL = 1024
N_HEADS = 8
A_HEAD_DIM = 64
A_ROT_DIM = 16
MOBA_BLOCK = 256
MOBA_TOPK = 3
MAX_MOBA_BLOCKS = 16
B_NOPE_DIM = 64
B_ROPE_DIM = 32
B_QK_DIM = B_NOPE_DIM + B_ROPE_DIM
B_V_DIM = 64
B_Q_RANK = 384
B_KV_RANK = 256
ROPE_THETA = 500000.0
EPS = 1e-6
NEG = -1e30
D_FF = 2816
N_EXPERTS = 8

LANES = 128
SLOT = LANES
MXU_DIM = 256
HW = N_HEADS * SLOT
BIAS_LANE0 = A_HEAD_DIM
V_WIDTH = N_HEADS * B_V_DIM

C_QA, C_KA, C_VA = 0, HW, 2 * HW
C_CQ = C_VA + HW
C_CKV = C_CQ + B_Q_RANK
C_KR = C_CKV + B_KV_RANK
N_PROJ = C_KR + LANES
SUM_LANE = B_V_DIM
LOG2E = math.log2(math.e)

VMEM_LIMIT = 56 * 1024 * 1024


def _cparams(n_axes):
    return pltpu.CompilerParams(dimension_semantics=("arbitrary",) * n_axes,
                                vmem_limit_bytes=VMEM_LIMIT)


def _dot(a, b):
    return jnp.dot(a, b, preferred_element_type=F32)


def _dot_nt(a, b):
    return lax.dot_general(a, b, (((1,), (1,)), ((), ())), preferred_element_type=F32)


def _split_bf16(a):
    hi = a.astype(BF16)
    lo = (a - hi.astype(F32)).astype(BF16)
    return hi, lo


def _dot3(a, b):
    ah, al = _split_bf16(a)
    bh, bl = _split_bf16(b)
    return _dot(ah, bh) + (_dot(ah, bl) + _dot(al, bh))


def _dot3_nt(a, b):
    ah, al = _split_bf16(a)
    bh, bl = _split_bf16(b)
    return _dot_nt(ah, bh) + (_dot_nt(ah, bl) + _dot_nt(al, bh))


def _rms_rows(xf, g):
    ms = jnp.mean(xf * xf, axis=-1, keepdims=True)
    return xf * lax.rsqrt(ms + EPS) * g


def _sigmoid(z):
    return 1.0 / (1.0 + jnp.exp(-z))


def _slot_norm_rope(x, g_ref, seg_ref, inv_n, rope_ref, shift):
    cos = rope_ref[0]
    sin_fwd = rope_ref[1]
    sin_bwd = rope_ref[2]
    slots = []
    for c in range(HW // MXU_DIM):
        xc = x[:, c * MXU_DIM:(c + 1) * MXU_DIM]
        ss = _dot((xc * xc).astype(BF16), seg_ref[...])
        xn = xc * lax.rsqrt(ss * inv_n + EPS) * g_ref[:, c * MXU_DIM:(c + 1) * MXU_DIM]
        for j in range(MXU_DIM // SLOT):
            xs = xn[:, j * SLOT:(j + 1) * SLOT]
            slots.append(xs * cos + pltpu.roll(xs, shift, 1) * sin_fwd
                         + pltpu.roll(xs, SLOT - shift, 1) * sin_bwd)
    return slots


def _prologue_kernel(x_ref, an_ref, w_ref, wuq_ref, wukv_ref, cqn_ref, ckvn_ref,
                     qna_ref, kna_ref, qnb_ref, knb_ref, seg_ref, place_ref,
                     ropea_ref, ropeb_ref,
                     qa_ref, ka_ref, va_ref, qb_ref, kb_ref, vb_ref, kmean_scr):
    s = pl.program_id(1)
    tm = x_ref.shape[1]

    @pl.when(s == 0)
    def _():
        kmean_scr[...] = jnp.zeros_like(kmean_scr)

    hn = _rms_rows(x_ref[0], an_ref[...]).astype(BF16)
    proj = _dot(hn, w_ref[...])

    qa = _slot_norm_rope(proj[:, C_QA:C_QA + HW], qna_ref, seg_ref, 1.0 / A_HEAD_DIM,
                         ropea_ref, A_ROT_DIM // 2)
    ka = _slot_norm_rope(proj[:, C_KA:C_KA + HW], kna_ref, seg_ref, 1.0 / A_HEAD_DIM,
                         ropea_ref, A_ROT_DIM // 2)

    kmean = kmean_scr[...]
    gates = [_dot3_nt(kmean[:, h * SLOT:(h + 1) * SLOT], qa[h]) for h in range(N_HEADS)]
    g3 = jnp.stack(gates, axis=0)
    blk = lax.broadcasted_iota(jnp.int32, g3.shape, 1)
    g3 = jnp.where(blk < s, g3, -jnp.inf)
    rank3 = jnp.zeros(g3.shape, F32)
    for n in range(MAX_MOBA_BLOCKS):
        row = g3[:, n:n + 1, :]
        rank3 = rank3 + jnp.where((row > g3) | ((row == g3) & (blk > n)), 1.0, 0.0)
    sel3 = jnp.where(((rank3 < MOBA_TOPK) & (blk < s)) | (blk == s), 1.0, 0.0)
    sel_t = sel3.reshape(N_HEADS * MAX_MOBA_BLOCKS, tm)
    placed = _dot(sel_t.T.astype(BF16), place_ref[...])
    lane = lax.broadcasted_iota(jnp.int32, (tm, SLOT), 1)
    is_bias = (lane >= BIAS_LANE0) & (lane < BIAS_LANE0 + MAX_MOBA_BLOCKS)
    scale_a = LOG2E / math.sqrt(A_HEAD_DIM)
    for h in range(N_HEADS):
        sl = slice(h * SLOT, (h + 1) * SLOT)
        bias = (placed[:, sl] - 1.0) * (-NEG)
        qa_ref[0, :, sl] = jnp.where(is_bias, bias, qa[h] * scale_a).astype(BF16)
        ka_ref[0, :, sl] = jnp.where(lane == BIAS_LANE0 + s, 1.0, ka[h]).astype(BF16)
        va_ref[0, :, sl] = jnp.where(lane == SUM_LANE, 1.0,
                                     proj[:, C_VA + h * SLOT:C_VA + (h + 1) * SLOT]).astype(BF16)

    km = jnp.concatenate([jnp.mean(k, axis=0, keepdims=True) for k in ka], axis=1)
    kmean_scr[pl.ds(s, 1), :] = km

    cq = _rms_rows(proj[:, C_CQ:C_CQ + B_Q_RANK], cqn_ref[...]).astype(BF16)
    qb = _dot(cq, wuq_ref[...])
    ckv = _rms_rows(proj[:, C_CKV:C_CKV + B_KV_RANK], ckvn_ref[...]).astype(BF16)
    kv = _dot(ckv, wukv_ref[...])
    kr = pltpu.roll(proj[:, C_KR:C_KR + LANES], B_NOPE_DIM, 1)
    kb = kv[:, :HW] + jnp.concatenate([kr] * N_HEADS, axis=1)
    scale_b = LOG2E / math.sqrt(B_QK_DIM)
    qb_s = _slot_norm_rope(qb, qnb_ref, seg_ref, 1.0 / B_QK_DIM, ropeb_ref, B_ROPE_DIM // 2)
    kb_s = _slot_norm_rope(kb, knb_ref, seg_ref, 1.0 / B_QK_DIM, ropeb_ref, B_ROPE_DIM // 2)
    for h in range(N_HEADS):
        sl = slice(h * SLOT, (h + 1) * SLOT)
        qb_ref[0, :, sl] = (qb_s[h] * scale_b).astype(BF16)
        kb_ref[0, :, sl] = kb_s[h].astype(BF16)
        vb_ref[0, :, sl] = jnp.where(lane == SUM_LANE, 1.0,
                                     kv[:, HW + h * SLOT:HW + (h + 1) * SLOT]).astype(BF16)


def _prologue(x, p, rope_a, rope_b):
    b, s, _ = x.shape
    tm = MOBA_BLOCK
    assert s % tm == 0 and s // tm <= MAX_MOBA_BLOCKS
    full = lambda a: pl.BlockSpec(a.shape, lambda i, j: (0,) * a.ndim)
    tok = lambda w: pl.BlockSpec((1, tm, w), lambda i, j: (i, j, 0))
    rope = pl.BlockSpec((3, tm, LANES), lambda i, j: (0, j, 0))
    consts = [p["attn_norm"], p["w_proj"], p["w_uq"], p["w_ukv"], p["cq_norm"], p["ckv_norm"],
              p["qn_a"], p["kn_a"], p["qn_b"], p["kn_b"], p["seg"], p["place"]]
    out_shape = [jax.ShapeDtypeStruct((b, s, HW), BF16)] * 6
    return pl.pallas_call(
        _prologue_kernel,
        grid=(b, s // tm),
        in_specs=[tok(D_MODEL)] + [full(a) for a in consts] + [rope, rope],
        out_specs=[tok(HW)] * 6,
        out_shape=out_shape,
        scratch_shapes=[pltpu.VMEM((MAX_MOBA_BLOCKS, HW), F32)],
        compiler_params=_cparams(2),
        name="prologue",
    )(x, *consts, rope_a, rope_b)


ATTN_TILE = 512
ATTN_HEADS = 4


def _attn_kernel(q_ref, k_ref, v_ref, o_ref):
    i = pl.program_id(2)
    t = q_ref.shape[1]
    row = lax.broadcasted_iota(jnp.int32, (t, t), 0)
    col = lax.broadcasted_iota(jnp.int32, (t, t), 1)
    causal = col <= row

    def scores(n, hh):
        sl = slice(hh * SLOT, (hh + 1) * SLOT)
        off = pl.multiple_of(n * t, t)
        return _dot_nt(q_ref[0, :, sl], k_ref[0, pl.ds(off, t), sl]), v_ref[0, pl.ds(off, t), sl]

    first = []
    for hh in range(ATTN_HEADS):
        sc, v = scores(i, hh)
        sc = jnp.where(causal, sc, NEG)
        m = jnp.max(sc, axis=1, keepdims=True)
        first.append((m, _dot(jnp.exp2(sc - m).astype(BF16), v)))

    def past_tile(n, state):
        new = []
        for hh, (m, acc) in enumerate(state):
            sc, v = scores(n, hh)
            m_new = jnp.maximum(m, jnp.max(sc, axis=1, keepdims=True))
            new.append((m_new, jnp.exp2(m - m_new) * acc + _dot(jnp.exp2(sc - m_new).astype(BF16), v)))
        return tuple(new)

    last = lax.fori_loop(0, i, past_tile, tuple(first))
    outs = [acc / acc[:, SUM_LANE:SUM_LANE + 1] for _, acc in last]
    lane = lax.broadcasted_iota(jnp.int32, (t, LANES), 1)
    for p in range(ATTN_HEADS // 2):
        pair = jnp.where(lane < B_V_DIM, outs[2 * p], pltpu.roll(outs[2 * p + 1], B_V_DIM, 1))
        o_ref[0, :, p * LANES:(p + 1) * LANES] = pair.astype(o_ref.dtype)


def _attention(q, k, v):
    b, s, _ = q.shape
    t = min(ATTN_TILE, s)
    assert s % t == 0 and t % MOBA_BLOCK == 0
    return pl.pallas_call(
        _attn_kernel,
        grid=(b, N_HEADS // ATTN_HEADS, s // t),
        in_specs=[pl.BlockSpec((1, t, ATTN_HEADS * SLOT), lambda bi, p, i: (bi, i, p)),
                  pl.BlockSpec((1, s, ATTN_HEADS * SLOT), lambda bi, p, i: (bi, 0, p)),
                  pl.BlockSpec((1, s, ATTN_HEADS * SLOT), lambda bi, p, i: (bi, 0, p))],
        out_specs=pl.BlockSpec((1, t, ATTN_HEADS * B_V_DIM), lambda bi, p, i: (bi, i, p)),
        out_shape=jax.ShapeDtypeStruct((b, s, V_WIDTH), BF16),
        compiler_params=_cparams(3),
        name="attention",
    )(q, k, v)


def _merge_kernel(x_ref, an_ref, ya_ref, yb_ref, wg_ref, wa_ref, wb_ref, wo_ref, o_ref):
    xf = x_ref[...]
    hn = _rms_rows(xf, an_ref[...]).astype(BF16)
    gates = _dot(hn, wg_ref[...])
    pa = _dot(ya_ref[...], wa_ref[...])
    pb = _dot(yb_ref[...], wb_ref[...])
    merged = _sigmoid(gates[:, :D_MODEL]) * pa + _sigmoid(gates[:, D_MODEL:]) * pb
    o_ref[...] = xf + _dot(merged.astype(BF16), wo_ref[...])


def _merge(x2, p, ya2, yb2, tm=512):
    t = x2.shape[0]
    full = lambda a: pl.BlockSpec(a.shape, lambda i: (0,) * a.ndim)
    tok = lambda w: pl.BlockSpec((tm, w), lambda i: (i, 0))
    consts = [p["w_gates"], p["w_br_a"], p["w_br_b"], p["w_out"]]
    return pl.pallas_call(
        _merge_kernel,
        grid=(t // tm,),
        in_specs=[tok(D_MODEL), full(p["attn_norm"]), tok(V_WIDTH), tok(V_WIDTH)] + [full(a) for a in consts],
        out_specs=tok(D_MODEL),
        out_shape=jax.ShapeDtypeStruct((t, D_MODEL), F32),
        compiler_params=_cparams(1),
        name="merge",
    )(x2, p["attn_norm"], ya2, yb2, *consts)


FF_CHUNK = MXU_DIM


def _ffn_kernel(x_ref, fn_ref, wgu_ref, wd_ref, o_ref, act_scr):
    xf = x_ref[...]
    h = _rms_rows(xf, fn_ref[...]).astype(BF16)
    for c in range(D_FF // FF_CHUNK):
        g = _dot(h, wgu_ref[:, c * FF_CHUNK:(c + 1) * FF_CHUNK])
        u = _dot(h, wgu_ref[:, D_FF + c * FF_CHUNK:D_FF + (c + 1) * FF_CHUNK])
        act_scr[:, c * FF_CHUNK:(c + 1) * FF_CHUNK] = (g * _sigmoid(g) * u).astype(BF16)
    o_ref[...] = xf + _dot(act_scr[...], wd_ref[...])


def _ffn_dense(x2, fn, w_gu, w_down, tm=512):
    t = x2.shape[0]
    full = lambda a: pl.BlockSpec(a.shape, lambda i: (0,) * a.ndim)
    tok = pl.BlockSpec((tm, D_MODEL), lambda i: (i, 0))
    return pl.pallas_call(
        _ffn_kernel,
        grid=(t // tm,),
        in_specs=[tok, full(fn), full(w_gu), full(w_down)],
        out_specs=tok,
        out_shape=jax.ShapeDtypeStruct((t, D_MODEL), F32),
        scratch_shapes=[pltpu.VMEM((tm, D_FF), BF16)],
        compiler_params=_cparams(1),
        name="ffn_dense",
    )(x2, fn, w_gu, w_down)


MOE_TILE = 512
ROUTE_TILE = 512
DMA_TOKENS = 256
PACKED = D_MODEL // 2
R_I1, R_I2, R_RANK1, R_RANK2, R_P1, R_P2 = range(6)


def _pack_bf16_pairs(xf):
    bits = lax.bitcast_convert_type(xf.astype(BF16).astype(F32), jnp.uint32)
    half = xf.shape[1] // 2
    return bits[:, :half] | (bits[:, half:] >> 16)


def _unpack_bf16_pairs(w):
    hi = lax.bitcast_convert_type(w & jnp.uint32(0xFFFF0000), F32)
    lo = lax.bitcast_convert_type(w << 16, F32)
    return jnp.concatenate([hi, lo], axis=1)


def _router_kernel(x_ref, fn_ref, rw_ref, hp_ref, route_ref, cnt_ref, carry_scr):
    tm = x_ref.shape[0]

    @pl.when(pl.program_id(0) == 0)
    def _():
        carry_scr[...] = jnp.zeros_like(carry_scr)

    hf = _rms_rows(x_ref[...], fn_ref[...])
    hp_ref[...] = _pack_bf16_pairs(hf)
    lane = lax.broadcasted_iota(jnp.int32, (tm, LANES), 1)
    logits = jnp.where(lane < N_EXPERTS, _dot3(hf, rw_ref[...]), -jnp.inf)
    m1 = jnp.max(logits, axis=1, keepdims=True)
    i1 = jnp.min(jnp.where(logits == m1, lane, LANES), axis=1, keepdims=True)
    rest = jnp.where(lane == i1, -jnp.inf, logits)
    m2 = jnp.max(rest, axis=1, keepdims=True)
    i2 = jnp.min(jnp.where(rest == m2, lane, LANES), axis=1, keepdims=True)
    e2 = jnp.exp(m2 - m1)
    p1 = 1.0 / (1.0 + e2)
    p2 = e2 / (1.0 + e2)
    onehot = jnp.where((lane == i1) | (lane == i2), 1.0, 0.0)
    r = lax.broadcasted_iota(jnp.int32, (tm, tm), 0)
    c = lax.broadcasted_iota(jnp.int32, (tm, tm), 1)
    earlier = jnp.where(c < r, 1.0, 0.0).astype(BF16)
    before = _dot(earlier, onehot.astype(BF16)) + carry_scr[0:1, :]
    rank1 = jnp.sum(jnp.where(lane == i1, before, 0.0), axis=1, keepdims=True)
    rank2 = jnp.sum(jnp.where(lane == i2, before, 0.0), axis=1, keepdims=True)
    rec = jnp.zeros((tm, LANES), F32)
    for pos, val in ((R_I1, i1.astype(F32)), (R_I2, i2.astype(F32)), (R_RANK1, rank1), (R_RANK2, rank2),
                     (R_P1, p1), (R_P2, p2)):
        rec = jnp.where(lane == pos, val, rec)
    route_ref[...] = rec
    total = carry_scr[0:1, :] + jnp.sum(onehot, axis=0, keepdims=True)
    carry_scr[...] = jnp.broadcast_to(total, carry_scr.shape)
    cnt_ref[...] = jnp.broadcast_to(total, cnt_ref.shape)


def _router(x2, fn, router_w):
    t = x2.shape[0]
    tm = ROUTE_TILE
    full = lambda a: pl.BlockSpec(a.shape, lambda i: (0,) * a.ndim)
    return pl.pallas_call(
        _router_kernel,
        grid=(t // tm,),
        in_specs=[pl.BlockSpec((tm, D_MODEL), lambda i: (i, 0)), full(fn), full(router_w)],
        out_specs=[pl.BlockSpec((tm, PACKED), lambda i: (i, 0)), pl.BlockSpec((tm, LANES), lambda i: (i, 0)),
                   pl.BlockSpec((8, LANES), lambda i: (0, 0))],
        out_shape=[jax.ShapeDtypeStruct((t, PACKED), jnp.uint32), jax.ShapeDtypeStruct((t, LANES), F32),
                   jax.ShapeDtypeStruct((8, LANES), F32)],
        scratch_shapes=[pltpu.VMEM((8, LANES), F32)],
        compiler_params=_cparams(1),
        name="moe_router",
    )(x2, fn, router_w)


def _row_copy(src_ref, src_row, dst_ref, dst_row, sem):
    return pltpu.make_async_copy(src_ref.at[pl.ds(src_row, 1)], dst_ref.at[pl.ds(dst_row, 1)], sem)


DMA_UNROLL = 8


def _scatter_kernel(dest_ref, hp_ref, xs_in_ref, xs_ref, stage, sem):
    del xs_in_ref
    c = pl.program_id(0)
    n = dest_ref.shape[2]
    slot = c % 2
    stage[slot] = hp_ref[...]

    def issue(t, carry):
        for k in range(2):
            _row_copy(stage.at[slot], t, xs_ref, dest_ref[0, k, t], sem.at[slot]).start(priority=k)
        return carry

    lax.fori_loop(0, n, issue, 0, unroll=DMA_UNROLL)

    def drain(s):
        def one(t, carry):
            _row_copy(stage.at[s], 0, xs_ref, 0, sem.at[s]).wait()
            return carry
        lax.fori_loop(0, 2 * n, one, 0, unroll=DMA_UNROLL)

    @pl.when(c > 0)
    def _():
        drain(1 - slot)

    @pl.when(c == pl.num_programs(0) - 1)
    def _():
        drain(slot)


def _scatter_rows(dest3, hp, n_rows):
    nc, _, n = dest3.shape
    xs0 = jnp.zeros((n_rows, PACKED), jnp.uint32)
    return pl.pallas_call(
        _scatter_kernel,
        grid=(nc,),
        in_specs=[pl.BlockSpec((1, 2, n), lambda c: (c, 0, 0), memory_space=pltpu.SMEM),
                  pl.BlockSpec((n, PACKED), lambda c: (c, 0)), pl.BlockSpec(memory_space=pl.ANY)],
        out_specs=pl.BlockSpec(memory_space=pl.ANY),
        out_shape=jax.ShapeDtypeStruct((n_rows, PACKED), jnp.uint32),
        scratch_shapes=[pltpu.VMEM((2, n, PACKED), jnp.uint32), pltpu.SemaphoreType.DMA((2,))],
        input_output_aliases={2: 0},
        compiler_params=_cparams(1),
        name="moe_scatter",
    )(dest3, hp, xs0)


def _experts_kernel(te_ref, tv_ref, xs_ref, wgu_ref, wd_ref, y_ref, act_scr):
    del te_ref
    j = pl.program_id(0)

    @pl.when(tv_ref[j] != 0)
    def _():
        h = _unpack_bf16_pairs(xs_ref[...]).astype(BF16)
        for c in range(D_FF // FF_CHUNK):
            g = _dot(h, wgu_ref[0, :, c * FF_CHUNK:(c + 1) * FF_CHUNK])
            u = _dot(h, wgu_ref[0, :, D_FF + c * FF_CHUNK:D_FF + (c + 1) * FF_CHUNK])
            act_scr[:, c * FF_CHUNK:(c + 1) * FF_CHUNK] = (g * _sigmoid(g) * u).astype(BF16)
        y_ref[...] = _pack_bf16_pairs(_dot(act_scr[...], wd_ref[0]))

    @pl.when(tv_ref[j] == 0)
    def _():
        y_ref[...] = jnp.zeros_like(y_ref)


def _experts(tile_expert, tile_valid, xs, w_gu, w_down):
    n_rows = xs.shape[0]
    tm = MOE_TILE
    row = pl.BlockSpec((tm, PACKED), lambda j, te, tv: (j, 0))
    return pl.pallas_call(
        _experts_kernel,
        grid_spec=pltpu.PrefetchScalarGridSpec(
            num_scalar_prefetch=2,
            grid=(n_rows // tm,),
            in_specs=[row,
                      pl.BlockSpec((1, D_MODEL, 2 * D_FF), lambda j, te, tv: (te[j], 0, 0)),
                      pl.BlockSpec((1, D_FF, D_MODEL), lambda j, te, tv: (te[j], 0, 0))],
            out_specs=row,
            scratch_shapes=[pltpu.VMEM((tm, D_FF), BF16)]),
        out_shape=jax.ShapeDtypeStruct((n_rows, PACKED), jnp.uint32),
        compiler_params=_cparams(1),
        name="moe_experts",
    )(tile_expert, tile_valid, xs, w_gu, w_down)


def _combine_kernel(dcur_ref, dnext_ref, x_ref, route_ref, y_ref, o_ref, ybuf, sem):
    c = pl.program_id(0)
    last = pl.num_programs(0) - 1
    n = dcur_ref.shape[2]
    slot = c % 2

    def issue(dref, s):
        def body(t, carry):
            for k in range(2):
                _row_copy(y_ref, dref[0, k, t], ybuf.at[s, k], t, sem.at[s]).start(priority=k)
            return carry
        lax.fori_loop(0, n, body, 0, unroll=DMA_UNROLL)

    @pl.when(c == 0)
    def _():
        issue(dcur_ref, 0)

    @pl.when(c < last)
    def _():
        issue(dnext_ref, 1 - slot)

    def one(t, carry):
        _row_copy(y_ref, 0, ybuf.at[slot, 0], 0, sem.at[slot]).wait()
        return carry
    lax.fori_loop(0, 2 * n, one, 0, unroll=DMA_UNROLL)

    rec = route_ref[...]
    y1 = _unpack_bf16_pairs(ybuf[slot, 0])
    y2 = _unpack_bf16_pairs(ybuf[slot, 1])
    o_ref[...] = x_ref[...] + (rec[:, R_P1:R_P1 + 1] * y1 + rec[:, R_P2:R_P2 + 1] * y2)


def _combine(dest3, x2, route, y):
    t = x2.shape[0]
    nc, _, n = dest3.shape
    tok = lambda w: pl.BlockSpec((n, w), lambda c: (c, 0))
    return pl.pallas_call(
        _combine_kernel,
        grid=(nc,),
        in_specs=[pl.BlockSpec((1, 2, n), lambda c: (c, 0, 0), memory_space=pltpu.SMEM),
                  pl.BlockSpec((1, 2, n), lambda c: (jnp.minimum(c + 1, nc - 1), 0, 0), memory_space=pltpu.SMEM),
                  tok(D_MODEL), tok(LANES), pl.BlockSpec(memory_space=pl.ANY)],
        out_specs=tok(D_MODEL),
        out_shape=jax.ShapeDtypeStruct((t, D_MODEL), F32),
        scratch_shapes=[pltpu.VMEM((2, 2, n, PACKED), jnp.uint32), pltpu.SemaphoreType.DMA((2,))],
        compiler_params=_cparams(1),
        name="moe_combine",
    )(dest3, dest3, x2, route, y)


def _ffn_moe(x2, fn, router_w, w_gu, w_down):
    t = x2.shape[0]
    hp, route, cnt = _router(x2, fn, router_w)
    i1 = route[:, R_I1].astype(jnp.int32)
    i2 = route[:, R_I2].astype(jnp.int32)
    counts = cnt[0, :N_EXPERTS].astype(jnp.int32)
    padded = (counts + MOE_TILE - 1) // MOE_TILE * MOE_TILE
    ends = jnp.cumsum(padded)
    offs = ends - padded
    experts = jnp.arange(N_EXPERTS, dtype=jnp.int32)
    off_of = lambda i: jnp.sum(jnp.where(i[:, None] == experts[None, :], offs[None, :], 0), axis=1)
    dest = jnp.stack([off_of(i1) + route[:, R_RANK1].astype(jnp.int32),
                      off_of(i2) + route[:, R_RANK2].astype(jnp.int32)])
    dest3 = dest.reshape(2, t // DMA_TOKENS, DMA_TOKENS).transpose(1, 0, 2)
    n_rows = 2 * t + N_EXPERTS * MOE_TILE
    tile_start = jnp.arange(n_rows // MOE_TILE, dtype=jnp.int32) * MOE_TILE
    tile_expert = jnp.minimum(jnp.sum((tile_start[:, None] >= ends[None, :]).astype(jnp.int32), axis=1),
                              N_EXPERTS - 1)
    tile_valid = (tile_start < ends[-1]).astype(jnp.int32)
    xs = _scatter_rows(dest3, hp, n_rows)
    y = _experts(tile_expert, tile_valid, xs, w_gu, w_down)
    return _combine(dest3, x2, route, y)


def _slot_cols(w, dims):
    k = w.shape[0]
    w = w.reshape(k, N_HEADS, dims)
    return jnp.pad(w, ((0, 0), (0, 0), (0, SLOT - dims))).reshape(k, HW)


def _slot_vec(g, dims):
    return jnp.tile(jnp.pad(g, (0, SLOT - dims)), N_HEADS).reshape(1, HW)


def _rope_table(seq, rot_dim, lane0):
    half = rot_dim // 2
    inv = ROPE_THETA ** (-jnp.arange(0, rot_dim, 2, dtype=F32) / rot_dim)
    ang = jnp.arange(seq, dtype=F32)[:, None] * inv[None, :]
    cos, sin = jnp.cos(ang), jnp.sin(ang)
    c = jnp.ones((seq, SLOT), F32).at[:, lane0:lane0 + half].set(cos).at[:, lane0 + half:lane0 + rot_dim].set(cos)
    s_fwd = jnp.zeros((seq, SLOT), F32).at[:, lane0 + half:lane0 + rot_dim].set(sin)
    s_bwd = jnp.zeros((seq, SLOT), F32).at[:, lane0:lane0 + half].set(-sin)
    return jnp.stack([c, s_fwd, s_bwd])


def _constants():
    r = np.arange(MXU_DIM)
    seg = (r[:, None] // SLOT == r[None, :] // SLOT).astype(np.float32)
    place = np.zeros((N_HEADS * MAX_MOBA_BLOCKS, HW), np.float32)
    for h in range(N_HEADS):
        for n in range(MAX_MOBA_BLOCKS):
            place[h * MAX_MOBA_BLOCKS + n, h * SLOT + BIAS_LANE0 + n] = 1.0
    return jnp.asarray(seg, BF16), jnp.asarray(place, BF16)


def _layer_params(l, attn_norm, w_in, moba_q_norm, moba_k_norm, mla_cq_norm, w_uq, mla_ckv_norm,
                  w_ukv, mla_q_norm, mla_k_norm, w_branch_a, w_branch_b, w_out):
    wi = w_in[l]
    aw = N_HEADS * A_HEAD_DIM
    o_cq = 3 * aw
    o_ckv = o_cq + B_Q_RANK
    o_kr = o_ckv + B_KV_RANK
    o_g = o_kr + B_ROPE_DIM
    w_proj = jnp.concatenate([
        _slot_cols(wi[:, 0:aw], A_HEAD_DIM), _slot_cols(wi[:, aw:2 * aw], A_HEAD_DIM),
        _slot_cols(wi[:, 2 * aw:3 * aw], A_HEAD_DIM),
        wi[:, o_cq:o_ckv], wi[:, o_ckv:o_kr], jnp.pad(wi[:, o_kr:o_g], ((0, 0), (0, LANES - B_ROPE_DIM)))], axis=1)
    ukv = w_ukv[l].reshape(B_KV_RANK, N_HEADS, B_NOPE_DIM + B_V_DIM)
    w_ukv_s = jnp.concatenate([_slot_cols(ukv[:, :, :B_NOPE_DIM].reshape(B_KV_RANK, -1), B_NOPE_DIM),
                               _slot_cols(ukv[:, :, B_NOPE_DIM:].reshape(B_KV_RANK, -1), B_V_DIM)], axis=1)
    seg, place = _constants()
    return {
        "attn_norm": attn_norm[l].reshape(1, D_MODEL),
        "w_proj": w_proj.astype(BF16),
        "w_gates": wi[:, o_g:].astype(BF16),
        "w_uq": _slot_cols(w_uq[l], B_QK_DIM).astype(BF16),
        "w_ukv": w_ukv_s.astype(BF16),
        "cq_norm": mla_cq_norm[l].reshape(1, B_Q_RANK),
        "ckv_norm": mla_ckv_norm[l].reshape(1, B_KV_RANK),
        "qn_a": _slot_vec(moba_q_norm[l], A_HEAD_DIM),
        "kn_a": _slot_vec(moba_k_norm[l], A_HEAD_DIM),
        "qn_b": _slot_vec(mla_q_norm[l], B_QK_DIM),
        "kn_b": _slot_vec(mla_k_norm[l], B_QK_DIM),
        "seg": seg,
        "place": place,
        "w_br_a": w_branch_a[l].astype(BF16),
        "w_br_b": w_branch_b[l].astype(BF16),
        "w_out": w_out[l].astype(BF16),
    }


def kernel(x, attn_norm, w_in, moba_q_norm, moba_k_norm, mla_cq_norm, w_uq, mla_ckv_norm, w_ukv,
           mla_q_norm, mla_k_norm, w_branch_a, w_branch_b, w_out, ffn_norm, dense_w_gate_up,
           dense_w_down, router_w, expert_w_gate_up, expert_w_down):
    b, s, d = x.shape
    depth = attn_norm.shape[0]
    rope_a = _rope_table(s, A_ROT_DIM, 0)
    rope_b = _rope_table(s, B_ROPE_DIM, B_NOPE_DIM)
    for l in range(depth):
        p = _layer_params(l, attn_norm, w_in, moba_q_norm, moba_k_norm, mla_cq_norm, w_uq,
                          mla_ckv_norm, w_ukv, mla_q_norm, mla_k_norm, w_branch_a, w_branch_b, w_out)
        qa, ka, va, qb, kb, vb = _prologue(x, p, rope_a, rope_b)
        ya = _attention(qa, ka, va)
        yb = _attention(qb, kb, vb)
        x2 = _merge(x.reshape(b * s, d), p, ya.reshape(b * s, V_WIDTH), yb.reshape(b * s, V_WIDTH))
        fn = ffn_norm[l].reshape(1, D_MODEL)
        if l % 2 == 0:
            x2 = _ffn_dense(x2, fn, dense_w_gate_up[l // 2].astype(BF16), dense_w_down[l // 2].astype(BF16))
        else:
            rw = jnp.pad(router_w[l // 2], ((0, 0), (0, LANES - N_EXPERTS)))
            x2 = _ffn_moe(x2, fn, rw, expert_w_gate_up[l // 2].astype(BF16), expert_w_down[l // 2].astype(BF16))
        x = x2.reshape(b, s, d)
    return x
```

```python
import math

import jax
import jax.numpy as jnp
import numpy as np
from jax import lax
from jax.experimental import pallas as pl
from jax.experimental.pallas import tpu as pltpu

F32 = jnp.float32
BF16 = jnp.bfloat16

D_MODEL = 1024
N_HEADS = 8
A_HEAD_DIM = 64
A_ROT_DIM = 16
MOBA_BLOCK = 256
MOBA_TOPK = 3
MAX_MOBA_BLOCKS = 16
B_NOPE_DIM = 64
B_ROPE_DIM = 32
B_QK_DIM = B_NOPE_DIM + B_ROPE_DIM
B_V_DIM = 64
B_Q_RANK = 384
B_KV_RANK = 256
ROPE_THETA = 500000.0
EPS = 1e-6
NEG = -1e30
D_FF = 2816
N_EXPERTS = 8

LANES = 128
SLOT = LANES
MXU_DIM = 256
HW = N_HEADS * SLOT
BIAS_LANE0 = A_HEAD_DIM
V_WIDTH = N_HEADS * B_V_DIM

C_QA, C_KA, C_VA = 0, HW, 2 * HW
C_CQ = C_VA + HW
C_CKV = C_CQ + B_Q_RANK
C_KR = C_CKV + B_KV_RANK
N_PROJ = C_KR + LANES
SUM_LANE = B_V_DIM
LOG2E = math.log2(math.e)

VMEM_LIMIT = 56 * 1024 * 1024


def _cparams(n_axes):
    return pltpu.CompilerParams(dimension_semantics=("arbitrary",) * n_axes,
                                vmem_limit_bytes=VMEM_LIMIT)


def _dot(a, b):
    return jnp.dot(a, b, preferred_element_type=F32)


def _dot_nt(a, b):
    return lax.dot_general(a, b, (((1,), (1,)), ((), ())), preferred_element_type=F32)


def _split_bf16(a):
    hi = a.astype(BF16)
    lo = (a - hi.astype(F32)).astype(BF16)
    return hi, lo


def _dot3(a, b):
    ah, al = _split_bf16(a)
    bh, bl = _split_bf16(b)
    return _dot(ah, bh) + (_dot(ah, bl) + _dot(al, bh))


def _dot3_nt(a, b):
    ah, al = _split_bf16(a)
    bh, bl = _split_bf16(b)
    return _dot_nt(ah, bh) + (_dot_nt(ah, bl) + _dot_nt(al, bh))


def _rms_rows(xf, g):
    ms = jnp.mean(xf * xf, axis=-1, keepdims=True)
    return xf * lax.rsqrt(ms + EPS) * g


def _sigmoid(z):
    return 1.0 / (1.0 + jnp.exp(-z))


PROLOGUE_BLOCKS = 2


def _slot_norm_rope(x, g_ref, seg_ref, inv_n, rope, shift):
    cos, sin_fwd, sin_bwd = rope
    slots = []
    for c in range(HW // MXU_DIM):
        xc = x[:, c * MXU_DIM:(c + 1) * MXU_DIM]
        ss = _dot((xc * xc).astype(BF16), seg_ref[...])
        xn = xc * lax.rsqrt(ss * inv_n + EPS) * g_ref[:, c * MXU_DIM:(c + 1) * MXU_DIM]
        for j in range(MXU_DIM // SLOT):
            xs = xn[:, j * SLOT:(j + 1) * SLOT]
            slots.append(xs * cos + pltpu.roll(xs, shift, 1) * sin_fwd
                         + pltpu.roll(xs, SLOT - shift, 1) * sin_bwd)
    return slots


def _prologue_kernel(x_ref, an_ref, w_ref, wuq_ref, wukv_ref, cqn_ref, ckvn_ref,
                     qna_ref, kna_ref, qnb_ref, knb_ref, seg_ref, place_ref,
                     ropea_ref, ropeb_ref,
                     qa_ref, ka_ref, va_ref, qb_ref, kb_ref, vb_ref, kmean_scr):
    @pl.when(pl.program_id(1) == 0)
    def _():
        kmean_scr[...] = jnp.zeros_like(kmean_scr)

    out_refs = (qa_ref, ka_ref, va_ref, qb_ref, kb_ref, vb_ref)
    for j in range(x_ref.shape[1] // MOBA_BLOCK):
        rows = slice(j * MOBA_BLOCK, (j + 1) * MOBA_BLOCK)
        hn = _rms_rows(x_ref[0, rows], an_ref[...]).astype(BF16)
        proj = _dot(hn, w_ref[...])
        _prologue_block(proj, pl.program_id(1) * (x_ref.shape[1] // MOBA_BLOCK) + j, rows,
                        wuq_ref, wukv_ref, cqn_ref, ckvn_ref, qna_ref, kna_ref, qnb_ref, knb_ref,
                        seg_ref, place_ref, [ropea_ref[t, rows] for t in range(3)],
                        [ropeb_ref[t, rows] for t in range(3)], out_refs, kmean_scr)


def _prologue_block(proj, s, rows, wuq_ref, wukv_ref, cqn_ref, ckvn_ref, qna_ref, kna_ref, qnb_ref, knb_ref,
                    seg_ref, place_ref, rope_a, rope_b, out_refs, kmean_scr):
    qa_ref, ka_ref, va_ref, qb_ref, kb_ref, vb_ref = out_refs
    tm = MOBA_BLOCK

    qa = _slot_norm_rope(proj[:, C_QA:C_QA + HW], qna_ref, seg_ref, 1.0 / A_HEAD_DIM, rope_a, A_ROT_DIM // 2)
    ka = _slot_norm_rope(proj[:, C_KA:C_KA + HW], kna_ref, seg_ref, 1.0 / A_HEAD_DIM, rope_a, A_ROT_DIM // 2)

    kmean = kmean_scr[...]
    gates = [_dot3_nt(kmean[:, h * SLOT:(h + 1) * SLOT], qa[h]) for h in range(N_HEADS)]
    g3 = jnp.stack(gates, axis=0)
    blk = lax.broadcasted_iota(jnp.int32, g3.shape, 1)
    g3 = jnp.where(blk < s, g3, -jnp.inf)
    rank3 = jnp.zeros(g3.shape, F32)
    for n in range(MAX_MOBA_BLOCKS):
        row = g3[:, n:n + 1, :]
        rank3 = rank3 + jnp.where((row > g3) | ((row == g3) & (blk > n)), 1.0, 0.0)
    sel3 = jnp.where(((rank3 < MOBA_TOPK) & (blk < s)) | (blk == s), 1.0, 0.0)
    sel_t = sel3.reshape(N_HEADS * MAX_MOBA_BLOCKS, tm)
    placed = _dot(sel_t.T.astype(BF16), place_ref[...])
    lane = lax.broadcasted_iota(jnp.int32, (tm, SLOT), 1)
    is_bias = (lane >= BIAS_LANE0) & (lane < BIAS_LANE0 + MAX_MOBA_BLOCKS)
    scale_a = LOG2E / math.sqrt(A_HEAD_DIM)
    for h in range(N_HEADS):
        sl = slice(h * SLOT, (h + 1) * SLOT)
        bias = (placed[:, sl] - 1.0) * (-NEG)
        qa_ref[0, rows, sl] = jnp.where(is_bias, bias, qa[h] * scale_a).astype(BF16)
        ka_ref[0, rows, sl] = jnp.where(lane == BIAS_LANE0 + s, 1.0, ka[h]).astype(BF16)
        va_ref[0, rows, sl] = jnp.where(lane == SUM_LANE, 1.0,
                                     proj[:, C_VA + h * SLOT:C_VA + (h + 1) * SLOT]).astype(BF16)

    km = jnp.concatenate([jnp.mean(k, axis=0, keepdims=True) for k in ka], axis=1)
    kmean_scr[pl.ds(s, 1), :] = km

    cq = _rms_rows(proj[:, C_CQ:C_CQ + B_Q_RANK], cqn_ref[...]).astype(BF16)
    qb = _dot(cq, wuq_ref[...])
    ckv = _rms_rows(proj[:, C_CKV:C_CKV + B_KV_RANK], ckvn_ref[...]).astype(BF16)
    kv = _dot(ckv, wukv_ref[...])
    kr = pltpu.roll(proj[:, C_KR:C_KR + LANES], B_NOPE_DIM, 1)
    kb = kv[:, :HW] + jnp.concatenate([kr] * N_HEADS, axis=1)
    scale_b = LOG2E / math.sqrt(B_QK_DIM)
    qb_s = _slot_norm_rope(qb, qnb_ref, seg_ref, 1.0 / B_QK_DIM, rope_b, B_ROPE_DIM // 2)
    kb_s = _slot_norm_rope(kb, knb_ref, seg_ref, 1.0 / B_QK_DIM, rope_b, B_ROPE_DIM // 2)
    for h in range(N_HEADS):
        sl = slice(h * SLOT, (h + 1) * SLOT)
        qb_ref[0, rows, sl] = (qb_s[h] * scale_b).astype(BF16)
        kb_ref[0, rows, sl] = kb_s[h].astype(BF16)
        vb_ref[0, rows, sl] = jnp.where(lane == SUM_LANE, 1.0,
                                     kv[:, HW + h * SLOT:HW + (h + 1) * SLOT]).astype(BF16)


def _prologue(x, p, rope_a, rope_b):
    b, s, _ = x.shape
    tm = PROLOGUE_BLOCKS * MOBA_BLOCK
    assert s % tm == 0 and s // MOBA_BLOCK <= MAX_MOBA_BLOCKS
    full = lambda a: pl.BlockSpec(a.shape, lambda i, j: (0,) * a.ndim)
    tok = lambda w: pl.BlockSpec((1, tm, w), lambda i, j: (i, j, 0))
    rope = pl.BlockSpec((3, tm, LANES), lambda i, j: (0, j, 0))
    consts = [p["attn_norm"], p["w_proj"], p["w_uq"], p["w_ukv"], p["cq_norm"], p["ckv_norm"],
              p["qn_a"], p["kn_a"], p["qn_b"], p["kn_b"], p["seg"], p["place"]]
    out_shape = [jax.ShapeDtypeStruct((b, s, HW), BF16)] * 6
    return pl.pallas_call(
        _prologue_kernel,
        grid=(b, s // tm),
        in_specs=[tok(D_MODEL)] + [full(a) for a in consts] + [rope, rope],
        out_specs=[tok(HW)] * 6,
        out_shape=out_shape,
        scratch_shapes=[pltpu.VMEM((MAX_MOBA_BLOCKS, HW), F32)],
        compiler_params=_cparams(2),
        name="prologue",
    )(x, *consts, rope_a, rope_b)


ATTN_TILE = 1024
ATTN_HEADS = 2
DIAG_SPLIT = 2


def _attn_kernel(q_ref, k_ref, v_ref, o_ref, m_scr, acc_scr):
    i = pl.program_id(2)
    t = q_ref.shape[1]
    tr = t // DIAG_SPLIT

    def scores(n, hh):
        sl = slice(hh * SLOT, (hh + 1) * SLOT)
        off = pl.multiple_of(n * t, t)
        return _dot_nt(q_ref[0, :, sl], k_ref[0, pl.ds(off, t), sl]), v_ref[0, pl.ds(off, t), sl]

    diag = pl.multiple_of(i * t, t)
    for hh in range(ATTN_HEADS):
        sl = slice(hh * SLOT, (hh + 1) * SLOT)
        for r in range(DIAG_SPLIT):
            rows = slice(r * tr, (r + 1) * tr)
            nk = (r + 1) * tr
            sc = _dot_nt(q_ref[0, rows, sl], k_ref[0, pl.ds(diag, nk), sl])
            row = lax.broadcasted_iota(jnp.int32, (tr, nk), 0)
            col = lax.broadcasted_iota(jnp.int32, (tr, nk), 1)
            sc = jnp.where(col <= row + r * tr, sc, NEG)
            m = jnp.max(sc, axis=1, keepdims=True)
            m_scr[hh, rows] = m
            acc_scr[hh, rows] = _dot(jnp.exp2(sc - m).astype(BF16), v_ref[0, pl.ds(diag, nk), sl])
    first = [(jnp.max(jnp.broadcast_to(m_scr[hh], (t, LANES)), axis=1, keepdims=True), acc_scr[hh])
             for hh in range(ATTN_HEADS)]

    def past_tile(n, state):
        new = []
        for hh, (m, acc) in enumerate(state):
            sc, v = scores(n, hh)
            m_new = jnp.maximum(m, jnp.max(sc, axis=1, keepdims=True))
            new.append((m_new, jnp.exp2(m - m_new) * acc + _dot(jnp.exp2(sc - m_new).astype(BF16), v)))
        return tuple(new)

    last = lax.fori_loop(0, i, past_tile, tuple(first))
    outs = [acc / acc[:, SUM_LANE:SUM_LANE + 1] for _, acc in last]
    lane = lax.broadcasted_iota(jnp.int32, (t, LANES), 1)
    for p in range(ATTN_HEADS // 2):
        pair = jnp.where(lane < B_V_DIM, outs[2 * p], pltpu.roll(outs[2 * p + 1], B_V_DIM, 1))
        o_ref[0, :, p * LANES:(p + 1) * LANES] = pair.astype(o_ref.dtype)


def _attention(q, k, v):
    b, s, _ = q.shape
    t = min(ATTN_TILE, s)
    assert s % t == 0 and t % MOBA_BLOCK == 0
    return pl.pallas_call(
        _attn_kernel,
        grid=(b, N_HEADS // ATTN_HEADS, s // t),
        in_specs=[pl.BlockSpec((1, t, ATTN_HEADS * SLOT), lambda bi, p, i: (bi, i, p)),
                  pl.BlockSpec((1, s, ATTN_HEADS * SLOT), lambda bi, p, i: (bi, 0, p)),
                  pl.BlockSpec((1, s, ATTN_HEADS * SLOT), lambda bi, p, i: (bi, 0, p))],
        out_specs=pl.BlockSpec((1, t, ATTN_HEADS * B_V_DIM), lambda bi, p, i: (bi, i, p)),
        out_shape=jax.ShapeDtypeStruct((b, s, V_WIDTH), BF16),
        scratch_shapes=[pltpu.VMEM((ATTN_HEADS, t, 1), F32), pltpu.VMEM((ATTN_HEADS, t, LANES), F32)],
        compiler_params=_cparams(3),
        name="attention",
    )(q, k, v)


def _merge_kernel(x_ref, an_ref, ya_ref, yb_ref, wg_ref, wa_ref, wb_ref, wo_ref, o_ref):
    xf = x_ref[...]
    hn = _rms_rows(xf, an_ref[...]).astype(BF16)
    gates = _dot(hn, wg_ref[...])
    pa = _dot(ya_ref[...], wa_ref[...])
    pb = _dot(yb_ref[...], wb_ref[...])
    merged = _sigmoid(gates[:, :D_MODEL]) * pa + _sigmoid(gates[:, D_MODEL:]) * pb
    o_ref[...] = xf + _dot(merged.astype(BF16), wo_ref[...])


def _merge(x2, p, ya2, yb2, tm=512):
    t = x2.shape[0]
    full = lambda a: pl.BlockSpec(a.shape, lambda i: (0,) * a.ndim)
    tok = lambda w: pl.BlockSpec((tm, w), lambda i: (i, 0))
    consts = [p["w_gates"], p["w_br_a"], p["w_br_b"], p["w_out"]]
    return pl.pallas_call(
        _merge_kernel,
        grid=(t // tm,),
        in_specs=[tok(D_MODEL), full(p["attn_norm"]), tok(V_WIDTH), tok(V_WIDTH)] + [full(a) for a in consts],
        out_specs=tok(D_MODEL),
        out_shape=jax.ShapeDtypeStruct((t, D_MODEL), F32),
        compiler_params=_cparams(1),
        name="merge",
    )(x2, p["attn_norm"], ya2, yb2, *consts)


FF_CHUNK = MXU_DIM


def _ffn_kernel(x_ref, fn_ref, wgu_ref, wd_ref, o_ref, act_scr):
    xf = x_ref[...]
    h = _rms_rows(xf, fn_ref[...]).astype(BF16)
    for c in range(D_FF // FF_CHUNK):
        g = _dot(h, wgu_ref[:, c * FF_CHUNK:(c + 1) * FF_CHUNK])
        u = _dot(h, wgu_ref[:, D_FF + c * FF_CHUNK:D_FF + (c + 1) * FF_CHUNK])
        act_scr[:, c * FF_CHUNK:(c + 1) * FF_CHUNK] = (g * _sigmoid(g) * u).astype(BF16)
    o_ref[...] = xf + _dot(act_scr[...], wd_ref[...])


def _ffn_dense(x2, fn, w_gu, w_down, tm=512):
    t = x2.shape[0]
    full = lambda a: pl.BlockSpec(a.shape, lambda i: (0,) * a.ndim)
    tok = pl.BlockSpec((tm, D_MODEL), lambda i: (i, 0))
    return pl.pallas_call(
        _ffn_kernel,
        grid=(t // tm,),
        in_specs=[tok, full(fn), full(w_gu), full(w_down)],
        out_specs=tok,
        out_shape=jax.ShapeDtypeStruct((t, D_MODEL), F32),
        scratch_shapes=[pltpu.VMEM((tm, D_FF), BF16)],
        compiler_params=_cparams(1),
        name="ffn_dense",
    )(x2, fn, w_gu, w_down)


MOE_TILE = 512
ROUTE_TILE = 512
DMA_TOKENS = 256
PACKED = D_MODEL // 2
R_I1, R_I2, R_RANK1, R_RANK2, R_P1, R_P2 = range(6)


def _pack_bf16_pairs(xf):
    bits = lax.bitcast_convert_type(xf.astype(BF16).astype(F32), jnp.uint32)
    half = xf.shape[1] // 2
    return bits[:, :half] | (bits[:, half:] >> 16)


def _unpack_bf16_pairs(w):
    hi = lax.bitcast_convert_type(w & jnp.uint32(0xFFFF0000), F32)
    lo = lax.bitcast_convert_type(w << 16, F32)
    return jnp.concatenate([hi, lo], axis=1)


def _router_kernel(x_ref, fn_ref, rw_ref, hp_ref, route_ref, cnt_ref, carry_scr):
    tm = x_ref.shape[0]

    @pl.when(pl.program_id(0) == 0)
    def _():
        carry_scr[...] = jnp.zeros_like(carry_scr)

    hf = _rms_rows(x_ref[...], fn_ref[...])
    hp_ref[...] = _pack_bf16_pairs(hf)
    lane = lax.broadcasted_iota(jnp.int32, (tm, LANES), 1)
    logits = jnp.where(lane < N_EXPERTS, _dot3(hf, rw_ref[...]), -jnp.inf)
    m1 = jnp.max(logits, axis=1, keepdims=True)
    i1 = jnp.min(jnp.where(logits == m1, lane, LANES), axis=1, keepdims=True)
    rest = jnp.where(lane == i1, -jnp.inf, logits)
    m2 = jnp.max(rest, axis=1, keepdims=True)
    i2 = jnp.min(jnp.where(rest == m2, lane, LANES), axis=1, keepdims=True)
    e2 = jnp.exp(m2 - m1)
    p1 = 1.0 / (1.0 + e2)
    p2 = e2 / (1.0 + e2)
    onehot = jnp.where((lane == i1) | (lane == i2), 1.0, 0.0)
    r = lax.broadcasted_iota(jnp.int32, (tm, tm), 0)
    c = lax.broadcasted_iota(jnp.int32, (tm, tm), 1)
    earlier = jnp.where(c < r, 1.0, 0.0).astype(BF16)
    before = _dot(earlier, onehot.astype(BF16)) + carry_scr[0:1, :]
    rank1 = jnp.sum(jnp.where(lane == i1, before, 0.0), axis=1, keepdims=True)
    rank2 = jnp.sum(jnp.where(lane == i2, before, 0.0), axis=1, keepdims=True)
    rec = jnp.zeros((tm, LANES), F32)
    for pos, val in ((R_I1, i1.astype(F32)), (R_I2, i2.astype(F32)), (R_RANK1, rank1), (R_RANK2, rank2),
                     (R_P1, p1), (R_P2, p2)):
        rec = jnp.where(lane == pos, val, rec)
    route_ref[...] = rec
    total = carry_scr[0:1, :] + jnp.sum(onehot, axis=0, keepdims=True)
    carry_scr[...] = jnp.broadcast_to(total, carry_scr.shape)
    cnt_ref[...] = jnp.broadcast_to(total, cnt_ref.shape)


def _router(x2, fn, router_w):
    t = x2.shape[0]
    tm = ROUTE_TILE
    full = lambda a: pl.BlockSpec(a.shape, lambda i: (0,) * a.ndim)
    return pl.pallas_call(
        _router_kernel,
        grid=(t // tm,),
        in_specs=[pl.BlockSpec((tm, D_MODEL), lambda i: (i, 0)), full(fn), full(router_w)],
        out_specs=[pl.BlockSpec((tm, PACKED), lambda i: (i, 0)), pl.BlockSpec((tm, LANES), lambda i: (i, 0)),
                   pl.BlockSpec((8, LANES), lambda i: (0, 0))],
        out_shape=[jax.ShapeDtypeStruct((t, PACKED), jnp.uint32), jax.ShapeDtypeStruct((t, LANES), F32),
                   jax.ShapeDtypeStruct((8, LANES), F32)],
        scratch_shapes=[pltpu.VMEM((8, LANES), F32)],
        compiler_params=_cparams(1),
        name="moe_router",
    )(x2, fn, router_w)


def _row_copy(src_ref, src_row, dst_ref, dst_row, sem):
    return pltpu.make_async_copy(src_ref.at[pl.ds(src_row, 1)], dst_ref.at[pl.ds(dst_row, 1)], sem)


DMA_UNROLL = 8


def _scatter_kernel(dest_ref, hp_ref, xs_in_ref, xs_ref, stage, sem):
    del xs_in_ref
    c = pl.program_id(0)
    n = dest_ref.shape[2]
    slot = c % 2
    stage[slot] = hp_ref[...]

    def issue(t, carry):
        for k in range(2):
            _row_copy(stage.at[slot], t, xs_ref, dest_ref[0, k, t], sem.at[slot]).start(priority=k)
        return carry

    lax.fori_loop(0, n, issue, 0, unroll=DMA_UNROLL)

    def drain(s):
        def one(t, carry):
            _row_copy(stage.at[s], 0, xs_ref, 0, sem.at[s]).wait()
            return carry
        lax.fori_loop(0, 2 * n, one, 0, unroll=DMA_UNROLL)

    @pl.when(c > 0)
    def _():
        drain(1 - slot)

    @pl.when(c == pl.num_programs(0) - 1)
    def _():
        drain(slot)


def _scatter_rows(dest3, hp, n_rows):
    nc, _, n = dest3.shape
    xs0 = jnp.zeros((n_rows, PACKED), jnp.uint32)
    return pl.pallas_call(
        _scatter_kernel,
        grid=(nc,),
        in_specs=[pl.BlockSpec((1, 2, n), lambda c: (c, 0, 0), memory_space=pltpu.SMEM),
                  pl.BlockSpec((n, PACKED), lambda c: (c, 0)), pl.BlockSpec(memory_space=pl.ANY)],
        out_specs=pl.BlockSpec(memory_space=pl.ANY),
        out_shape=jax.ShapeDtypeStruct((n_rows, PACKED), jnp.uint32),
        scratch_shapes=[pltpu.VMEM((2, n, PACKED), jnp.uint32), pltpu.SemaphoreType.DMA((2,))],
        input_output_aliases={2: 0},
        compiler_params=_cparams(1),
        name="moe_scatter",
    )(dest3, hp, xs0)


def _experts_kernel(te_ref, tv_ref, xs_ref, wgu_ref, wd_ref, y_ref, act_scr):
    del te_ref
    j = pl.program_id(0)

    @pl.when(tv_ref[j] != 0)
    def _():
        h = _unpack_bf16_pairs(xs_ref[...]).astype(BF16)
        for c in range(D_FF // FF_CHUNK):
            g = _dot(h, wgu_ref[0, :, c * FF_CHUNK:(c + 1) * FF_CHUNK])
            u = _dot(h, wgu_ref[0, :, D_FF + c * FF_CHUNK:D_FF + (c + 1) * FF_CHUNK])
            act_scr[:, c * FF_CHUNK:(c + 1) * FF_CHUNK] = (g * _sigmoid(g) * u).astype(BF16)
        y_ref[...] = _pack_bf16_pairs(_dot(act_scr[...], wd_ref[0]))

    @pl.when(tv_ref[j] == 0)
    def _():
        y_ref[...] = jnp.zeros_like(y_ref)


def _experts(tile_expert, tile_valid, xs, w_gu, w_down):
    n_rows = xs.shape[0]
    tm = MOE_TILE
    row = pl.BlockSpec((tm, PACKED), lambda j, te, tv: (j, 0))
    return pl.pallas_call(
        _experts_kernel,
        grid_spec=pltpu.PrefetchScalarGridSpec(
            num_scalar_prefetch=2,
            grid=(n_rows // tm,),
            in_specs=[row,
                      pl.BlockSpec((1, D_MODEL, 2 * D_FF), lambda j, te, tv: (te[j], 0, 0)),
                      pl.BlockSpec((1, D_FF, D_MODEL), lambda j, te, tv: (te[j], 0, 0))],
            out_specs=row,
            scratch_shapes=[pltpu.VMEM((tm, D_FF), BF16)]),
        out_shape=jax.ShapeDtypeStruct((n_rows, PACKED), jnp.uint32),
        compiler_params=_cparams(1),
        name="moe_experts",
    )(tile_expert, tile_valid, xs, w_gu, w_down)


def _combine_kernel(dcur_ref, dnext_ref, x_ref, route_ref, y_ref, o_ref, ybuf, sem):
    c = pl.program_id(0)
    last = pl.num_programs(0) - 1
    n = dcur_ref.shape[2]
    slot = c % 2

    def issue(dref, s):
        def body(t, carry):
            for k in range(2):
                _row_copy(y_ref, dref[0, k, t], ybuf.at[s, k], t, sem.at[s]).start(priority=k)
            return carry
        lax.fori_loop(0, n, body, 0, unroll=DMA_UNROLL)

    @pl.when(c == 0)
    def _():
        issue(dcur_ref, 0)

    @pl.when(c < last)
    def _():
        issue(dnext_ref, 1 - slot)

    def one(t, carry):
        _row_copy(y_ref, 0, ybuf.at[slot, 0], 0, sem.at[slot]).wait()
        return carry
    lax.fori_loop(0, 2 * n, one, 0, unroll=DMA_UNROLL)

    rec = route_ref[...]
    y1 = _unpack_bf16_pairs(ybuf[slot, 0])
    y2 = _unpack_bf16_pairs(ybuf[slot, 1])
    o_ref[...] = x_ref[...] + (rec[:, R_P1:R_P1 + 1] * y1 + rec[:, R_P2:R_P2 + 1] * y2)


def _combine(dest3, x2, route, y):
    t = x2.shape[0]
    nc, _, n = dest3.shape
    tok = lambda w: pl.BlockSpec((n, w), lambda c: (c, 0))
    return pl.pallas_call(
        _combine_kernel,
        grid=(nc,),
        in_specs=[pl.BlockSpec((1, 2, n), lambda c: (c, 0, 0), memory_space=pltpu.SMEM),
                  pl.BlockSpec((1, 2, n), lambda c: (jnp.minimum(c + 1, nc - 1), 0, 0), memory_space=pltpu.SMEM),
                  tok(D_MODEL), tok(LANES), pl.BlockSpec(memory_space=pl.ANY)],
        out_specs=tok(D_MODEL),
        out_shape=jax.ShapeDtypeStruct((t, D_MODEL), F32),
        scratch_shapes=[pltpu.VMEM((2, 2, n, PACKED), jnp.uint32), pltpu.SemaphoreType.DMA((2,))],
        compiler_params=_cparams(1),
        name="moe_combine",
    )(dest3, dest3, x2, route, y)


def _ffn_moe(x2, fn, router_w, w_gu, w_down):
    t = x2.shape[0]
    hp, route, cnt = _router(x2, fn, router_w)
    i1 = route[:, R_I1].astype(jnp.int32)
    i2 = route[:, R_I2].astype(jnp.int32)
    counts = cnt[0, :N_EXPERTS].astype(jnp.int32)
    padded = (counts + MOE_TILE - 1) // MOE_TILE * MOE_TILE
    ends = jnp.cumsum(padded)
    offs = ends - padded
    experts = jnp.arange(N_EXPERTS, dtype=jnp.int32)
    off_of = lambda i: jnp.sum(jnp.where(i[:, None] == experts[None, :], offs[None, :], 0), axis=1)
    dest = jnp.stack([off_of(i1) + route[:, R_RANK1].astype(jnp.int32),
                      off_of(i2) + route[:, R_RANK2].astype(jnp.int32)])
    dest3 = dest.reshape(2, t // DMA_TOKENS, DMA_TOKENS).transpose(1, 0, 2)
    n_rows = 2 * t + N_EXPERTS * MOE_TILE
    tile_start = jnp.arange(n_rows // MOE_TILE, dtype=jnp.int32) * MOE_TILE
    tile_expert = jnp.minimum(jnp.sum((tile_start[:, None] >= ends[None, :]).astype(jnp.int32), axis=1),
                              N_EXPERTS - 1)
    tile_valid = (tile_start < ends[-1]).astype(jnp.int32)
    xs = _scatter_rows(dest3, hp, n_rows)
    y = _experts(tile_expert, tile_valid, xs, w_gu, w_down)
    return _combine(dest3, x2, route, y)


def _slot_cols(w, dims):
    k = w.shape[0]
    w = w.reshape(k, N_HEADS, dims)
    return jnp.pad(w, ((0, 0), (0, 0), (0, SLOT - dims))).reshape(k, HW)


def _slot_vec(g, dims):
    return jnp.tile(jnp.pad(g, (0, SLOT - dims)), N_HEADS).reshape(1, HW)


def _rope_table(seq, rot_dim, lane0):
    half = rot_dim // 2
    inv = ROPE_THETA ** (-jnp.arange(0, rot_dim, 2, dtype=F32) / rot_dim)
    ang = jnp.arange(seq, dtype=F32)[:, None] * inv[None, :]
    cos, sin = jnp.cos(ang), jnp.sin(ang)
    c = jnp.ones((seq, SLOT), F32).at[:, lane0:lane0 + half].set(cos).at[:, lane0 + half:lane0 + rot_dim].set(cos)
    s_fwd = jnp.zeros((seq, SLOT), F32).at[:, lane0 + half:lane0 + rot_dim].set(sin)
    s_bwd = jnp.zeros((seq, SLOT), F32).at[:, lane0:lane0 + half].set(-sin)
    return jnp.stack([c, s_fwd, s_bwd])


def _constants():
    r = np.arange(MXU_DIM)
    seg = (r[:, None] // SLOT == r[None, :] // SLOT).astype(np.float32)
    place = np.zeros((N_HEADS * MAX_MOBA_BLOCKS, HW), np.float32)
    for h in range(N_HEADS):
        for n in range(MAX_MOBA_BLOCKS):
            place[h * MAX_MOBA_BLOCKS + n, h * SLOT + BIAS_LANE0 + n] = 1.0
    return jnp.asarray(seg, BF16), jnp.asarray(place, BF16)


def _layer_params(l, attn_norm, w_in, moba_q_norm, moba_k_norm, mla_cq_norm, w_uq, mla_ckv_norm,
                  w_ukv, mla_q_norm, mla_k_norm, w_branch_a, w_branch_b, w_out):
    wi = w_in[l]
    aw = N_HEADS * A_HEAD_DIM
    o_cq = 3 * aw
    o_ckv = o_cq + B_Q_RANK
    o_kr = o_ckv + B_KV_RANK
    o_g = o_kr + B_ROPE_DIM
    w_proj = jnp.concatenate([
        _slot_cols(wi[:, 0:aw], A_HEAD_DIM), _slot_cols(wi[:, aw:2 * aw], A_HEAD_DIM),
        _slot_cols(wi[:, 2 * aw:3 * aw], A_HEAD_DIM),
        wi[:, o_cq:o_ckv], wi[:, o_ckv:o_kr], jnp.pad(wi[:, o_kr:o_g], ((0, 0), (0, LANES - B_ROPE_DIM)))], axis=1)
    ukv = w_ukv[l].reshape(B_KV_RANK, N_HEADS, B_NOPE_DIM + B_V_DIM)
    w_ukv_s = jnp.concatenate([_slot_cols(ukv[:, :, :B_NOPE_DIM].reshape(B_KV_RANK, -1), B_NOPE_DIM),
                               _slot_cols(ukv[:, :, B_NOPE_DIM:].reshape(B_KV_RANK, -1), B_V_DIM)], axis=1)
    seg, place = _constants()
    return {
        "attn_norm": attn_norm[l].reshape(1, D_MODEL),
        "w_proj": w_proj.astype(BF16),
        "w_gates": wi[:, o_g:].astype(BF16),
        "w_uq": _slot_cols(w_uq[l], B_QK_DIM).astype(BF16),
        "w_ukv": w_ukv_s.astype(BF16),
        "cq_norm": mla_cq_norm[l].reshape(1, B_Q_RANK),
        "ckv_norm": mla_ckv_norm[l].reshape(1, B_KV_RANK),
        "qn_a": _slot_vec(moba_q_norm[l], A_HEAD_DIM),
        "kn_a": _slot_vec(moba_k_norm[l], A_HEAD_DIM),
        "qn_b": _slot_vec(mla_q_norm[l], B_QK_DIM),
        "kn_b": _slot_vec(mla_k_norm[l], B_QK_DIM),
        "seg": seg,
        "place": place,
        "w_br_a": w_branch_a[l].astype(BF16),
        "w_br_b": w_branch_b[l].astype(BF16),
        "w_out": w_out[l].astype(BF16),
    }


def kernel(x, attn_norm, w_in, moba_q_norm, moba_k_norm, mla_cq_norm, w_uq, mla_ckv_norm, w_ukv,
           mla_q_norm, mla_k_norm, w_branch_a, w_branch_b, w_out, ffn_norm, dense_w_gate_up,
           dense_w_down, router_w, expert_w_gate_up, expert_w_down):
    b, s, d = x.shape
    depth = attn_norm.shape[0]
    rope_a = _rope_table(s, A_ROT_DIM, 0)
    rope_b = _rope_table(s, B_ROPE_DIM, B_NOPE_DIM)
    for l in range(depth):
        p = _layer_params(l, attn_norm, w_in, moba_q_norm, moba_k_norm, mla_cq_norm, w_uq,
                          mla_ckv_norm, w_ukv, mla_q_norm, mla_k_norm, w_branch_a, w_branch_b, w_out)
        qa, ka, va, qb, kb, vb = _prologue(x, p, rope_a, rope_b)
        ya = _attention(qa, ka, va)
        yb = _attention(qb, kb, vb)
        x2 = _merge(x.reshape(b * s, d), p, ya.reshape(b * s, V_WIDTH), yb.reshape(b * s, V_WIDTH))
        fn = ffn_norm[l].reshape(1, D_MODEL)
        if l % 2 == 0:
            x2 = _ffn_dense(x2, fn, dense_w_gate_up[l // 2].astype(BF16), dense_w_down[l // 2].astype(BF16))
        else:
            rw = jnp.pad(router_w[l // 2], ((0, 0), (0, LANES - N_EXPERTS)))
            x2 = _ffn_moe(x2, fn, rw, expert_w_gate_up[l // 2].astype(BF16), expert_w_down[l // 2].astype(BF16))
        x = x2.reshape(b, s, d)
    return x
```

```python
import math

import jax
import jax.numpy as jnp
import numpy as np
from jax import lax
from jax.experimental import pallas as pl
from jax.experimental.pallas import tpu as pltpu

F32 = jnp.float32
BF16 = jnp.bfloat16

D_MODEL = 1024
N_HEADS = 8
A_HEAD_DIM = 64
A_ROT_DIM = 16
MOBA_BLOCK = 256
MOBA_TOPK = 3
MAX_MOBA_BLOCKS = 16
B_NOPE_DIM = 64
B_ROPE_DIM = 32
B_QK_DIM = B_NOPE_DIM + B_ROPE_DIM
B_V_DIM = 64
B_Q_RANK = 384
B_KV_RANK = 256
ROPE_THETA = 500000.0
EPS = 1e-6
NEG = -1e30
D_FF = 2816
N_EXPERTS = 8

LANES = 128
SLOT = LANES
MXU_DIM = 256
HW = N_HEADS * SLOT
BIAS_LANE0 = A_HEAD_DIM
V_WIDTH = N_HEADS * B_V_DIM

C_QA, C_KA, C_VA = 0, HW, 2 * HW
C_CQ = C_VA + HW
C_CKV = C_CQ + B_Q_RANK
C_KR = C_CKV + B_KV_RANK
N_PROJ = C_KR + LANES
SUM_LANE = B_V_DIM
LOG2E = math.log2(math.e)

VMEM_LIMIT = 56 * 1024 * 1024


def _cparams(n_axes):
    return pltpu.CompilerParams(dimension_semantics=("arbitrary",) * n_axes,
                                vmem_limit_bytes=VMEM_LIMIT)


def _dot(a, b):
    return jnp.dot(a, b, preferred_element_type=F32)


def _dot_nt(a, b):
    return lax.dot_general(a, b, (((1,), (1,)), ((), ())), preferred_element_type=F32)


def _split_bf16(a):
    hi = a.astype(BF16)
    lo = (a - hi.astype(F32)).astype(BF16)
    return hi, lo


def _dot3(a, b):
    ah, al = _split_bf16(a)
    bh, bl = _split_bf16(b)
    return _dot(ah, bh) + (_dot(ah, bl) + _dot(al, bh))


def _dot3_nt(a, b):
    ah, al = _split_bf16(a)
    bh, bl = _split_bf16(b)
    return _dot_nt(ah, bh) + (_dot_nt(ah, bl) + _dot_nt(al, bh))


def _rms_rows(xf, g):
    ms = jnp.mean(xf * xf, axis=-1, keepdims=True)
    return xf * lax.rsqrt(ms + EPS) * g


def _sigmoid(z):
    return 1.0 / (1.0 + jnp.exp(-z))


PROLOGUE_BLOCKS = 2


def _slot_norm_rope(x, g_ref, seg_ref, inv_n, rope, shift):
    cos, sin_fwd, sin_bwd = rope
    slots = []
    for c in range(HW // MXU_DIM):
        xc = x[:, c * MXU_DIM:(c + 1) * MXU_DIM]
        ss = _dot((xc * xc).astype(BF16), seg_ref[...])
        xn = xc * lax.rsqrt(ss * inv_n + EPS) * g_ref[:, c * MXU_DIM:(c + 1) * MXU_DIM]
        for j in range(MXU_DIM // SLOT):
            xs = xn[:, j * SLOT:(j + 1) * SLOT]
            slots.append(xs * cos + pltpu.roll(xs, shift, 1) * sin_fwd
                         + pltpu.roll(xs, SLOT - shift, 1) * sin_bwd)
    return slots


def _prologue_kernel(x_ref, an_ref, w_ref, wuq_ref, wukv_ref, cqn_ref, ckvn_ref,
                     qna_ref, kna_ref, qnb_ref, knb_ref, seg_ref, place_ref,
                     ropea_ref, ropeb_ref,
                     qa_ref, ka_ref, va_ref, qb_ref, kb_ref, vb_ref, kmean_scr):
    @pl.when(pl.program_id(1) == 0)
    def _():
        kmean_scr[...] = jnp.zeros_like(kmean_scr)

    out_refs = (qa_ref, ka_ref, va_ref, qb_ref, kb_ref, vb_ref)
    for j in range(x_ref.shape[1] // MOBA_BLOCK):
        rows = slice(j * MOBA_BLOCK, (j + 1) * MOBA_BLOCK)
        hn = _rms_rows(x_ref[0, rows], an_ref[...]).astype(BF16)
        proj = _dot(hn, w_ref[...])
        _prologue_block(proj, pl.program_id(1) * (x_ref.shape[1] // MOBA_BLOCK) + j, rows,
                        wuq_ref, wukv_ref, cqn_ref, ckvn_ref, qna_ref, kna_ref, qnb_ref, knb_ref,
                        seg_ref, place_ref, [ropea_ref[t, rows] for t in range(3)],
                        [ropeb_ref[t, rows] for t in range(3)], out_refs, kmean_scr)


def _prologue_block(proj, s, rows, wuq_ref, wukv_ref, cqn_ref, ckvn_ref, qna_ref, kna_ref, qnb_ref, knb_ref,
                    seg_ref, place_ref, rope_a, rope_b, out_refs, kmean_scr):
    qa_ref, ka_ref, va_ref, qb_ref, kb_ref, vb_ref = out_refs
    tm = MOBA_BLOCK

    qa = _slot_norm_rope(proj[:, C_QA:C_QA + HW], qna_ref, seg_ref, 1.0 / A_HEAD_DIM, rope_a, A_ROT_DIM // 2)
    ka = _slot_norm_rope(proj[:, C_KA:C_KA + HW], kna_ref, seg_ref, 1.0 / A_HEAD_DIM, rope_a, A_ROT_DIM // 2)

    kmean = kmean_scr[...]
    gates = [_dot3_nt(kmean[:, h * SLOT:(h + 1) * SLOT], qa[h]) for h in range(N_HEADS)]
    g3 = jnp.stack(gates, axis=0)
    blk = lax.broadcasted_iota(jnp.int32, g3.shape, 1)
    g3 = jnp.where(blk < s, g3, -jnp.inf)
    rank3 = jnp.zeros(g3.shape, F32)
    for n in range(MAX_MOBA_BLOCKS):
        row = g3[:, n:n + 1, :]
        rank3 = rank3 + jnp.where((row > g3) | ((row == g3) & (blk > n)), 1.0, 0.0)
    sel3 = jnp.where(((rank3 < MOBA_TOPK) & (blk < s)) | (blk == s), 1.0, 0.0)
    sel_t = sel3.reshape(N_HEADS * MAX_MOBA_BLOCKS, tm)
    placed = _dot(sel_t.T.astype(BF16), place_ref[...])
    lane = lax.broadcasted_iota(jnp.int32, (tm, SLOT), 1)
    is_bias = (lane >= BIAS_LANE0) & (lane < BIAS_LANE0 + MAX_MOBA_BLOCKS)
    scale_a = LOG2E / math.sqrt(A_HEAD_DIM)
    for h in range(N_HEADS):
        sl = slice(h * SLOT, (h + 1) * SLOT)
        bias = (placed[:, sl] - 1.0) * (-NEG)
        qa_ref[0, rows, sl] = jnp.where(is_bias, bias, qa[h] * scale_a).astype(BF16)
        ka_ref[0, rows, sl] = jnp.where(lane == BIAS_LANE0 + s, 1.0, ka[h]).astype(BF16)
        va_ref[0, rows, sl] = jnp.where(lane == SUM_LANE, 1.0,
                                        proj[:, C_VA + h * SLOT:C_VA + (h + 1) * SLOT]).astype(BF16)

    km = jnp.concatenate([jnp.mean(k, axis=0, keepdims=True) for k in ka], axis=1)
    kmean_scr[pl.ds(s, 1), :] = km

    cq = _rms_rows(proj[:, C_CQ:C_CQ + B_Q_RANK], cqn_ref[...]).astype(BF16)
    qb = _dot(cq, wuq_ref[...])
    ckv = _rms_rows(proj[:, C_CKV:C_CKV + B_KV_RANK], ckvn_ref[...]).astype(BF16)
    kv = _dot(ckv, wukv_ref[...])
    kr = pltpu.roll(proj[:, C_KR:C_KR + LANES], B_NOPE_DIM, 1)
    kb = kv[:, :HW] + jnp.concatenate([kr] * N_HEADS, axis=1)
    scale_b = LOG2E / math.sqrt(B_QK_DIM)
    qb_s = _slot_norm_rope(qb, qnb_ref, seg_ref, 1.0 / B_QK_DIM, rope_b, B_ROPE_DIM // 2)
    kb_s = _slot_norm_rope(kb, knb_ref, seg_ref, 1.0 / B_QK_DIM, rope_b, B_ROPE_DIM // 2)
    for h in range(N_HEADS):
        sl = slice(h * SLOT, (h + 1) * SLOT)
        qb_ref[0, rows, sl] = (qb_s[h] * scale_b).astype(BF16)
        kb_ref[0, rows, sl] = kb_s[h].astype(BF16)
        vb_ref[0, rows, sl] = jnp.where(lane == SUM_LANE, 1.0,
                                        kv[:, HW + h * SLOT:HW + (h + 1) * SLOT]).astype(BF16)


def _prologue(x, p, rope_a, rope_b):
    b, s, _ = x.shape
    tm = PROLOGUE_BLOCKS * MOBA_BLOCK
    assert s % tm == 0 and s // MOBA_BLOCK <= MAX_MOBA_BLOCKS
    full = lambda a: pl.BlockSpec(a.shape, lambda i, j: (0,) * a.ndim)
    tok = lambda w: pl.BlockSpec((1, tm, w), lambda i, j: (i, j, 0))
    rope = pl.BlockSpec((3, tm, LANES), lambda i, j: (0, j, 0))
    consts = [p["attn_norm"], p["w_proj"], p["w_uq"], p["w_ukv"], p["cq_norm"], p["ckv_norm"],
              p["qn_a"], p["kn_a"], p["qn_b"], p["kn_b"], p["seg"], p["place"]]
    out_shape = [jax.ShapeDtypeStruct((b, s, HW), BF16)] * 6
    return pl.pallas_call(
        _prologue_kernel,
        grid=(b, s // tm),
        in_specs=[tok(D_MODEL)] + [full(a) for a in consts] + [rope, rope],
        out_specs=[tok(HW)] * 6,
        out_shape=out_shape,
        scratch_shapes=[pltpu.VMEM((MAX_MOBA_BLOCKS, HW), F32)],
        compiler_params=_cparams(2),
        name="prologue",
    )(x, *consts, rope_a, rope_b)


ATTN_TILE = 1024
ATTN_HEADS = 2
DIAG_SPLIT = 2


def _attn_kernel(q_ref, k_ref, v_ref, o_ref, m_scr, acc_scr):
    i = pl.program_id(2)
    t = q_ref.shape[1]
    tr = t // DIAG_SPLIT

    def scores(n, hh):
        sl = slice(hh * SLOT, (hh + 1) * SLOT)
        off = pl.multiple_of(n * t, t)
        return _dot_nt(q_ref[0, :, sl], k_ref[0, pl.ds(off, t), sl]), v_ref[0, pl.ds(off, t), sl]

    diag = pl.multiple_of(i * t, t)
    for hh in range(ATTN_HEADS):
        sl = slice(hh * SLOT, (hh + 1) * SLOT)
        for r in range(DIAG_SPLIT):
            rows = slice(r * tr, (r + 1) * tr)
            nk = (r + 1) * tr
            sc = _dot_nt(q_ref[0, rows, sl], k_ref[0, pl.ds(diag, nk), sl])
            row = lax.broadcasted_iota(jnp.int32, (tr, nk), 0)
            col = lax.broadcasted_iota(jnp.int32, (tr, nk), 1)
            sc = jnp.where(col <= row + r * tr, sc, NEG)
            m = jnp.max(sc, axis=1, keepdims=True)
            m_scr[hh, rows] = m
            acc_scr[hh, rows] = _dot(jnp.exp2(sc - m).astype(BF16), v_ref[0, pl.ds(diag, nk), sl])
    first = [(jnp.max(jnp.broadcast_to(m_scr[hh], (t, LANES)), axis=1, keepdims=True), acc_scr[hh])
             for hh in range(ATTN_HEADS)]

    def past_tile(n, state):
        new = []
        for hh, (m, acc) in enumerate(state):
            sc, v = scores(n, hh)
            m_new = jnp.maximum(m, jnp.max(sc, axis=1, keepdims=True))
            new.append((m_new, jnp.exp2(m - m_new) * acc + _dot(jnp.exp2(sc - m_new).astype(BF16), v)))
        return tuple(new)

    last = lax.fori_loop(0, i, past_tile, tuple(first))
    outs = [acc / acc[:, SUM_LANE:SUM_LANE + 1] for _, acc in last]
    lane = lax.broadcasted_iota(jnp.int32, (t, LANES), 1)
    for p in range(ATTN_HEADS // 2):
        pair = jnp.where(lane < B_V_DIM, outs[2 * p], pltpu.roll(outs[2 * p + 1], B_V_DIM, 1))
        o_ref[0, :, p * LANES:(p + 1) * LANES] = pair.astype(o_ref.dtype)


def _attention(q, k, v):
    b, s, _ = q.shape
    t = min(ATTN_TILE, s)
    assert s % t == 0 and t % MOBA_BLOCK == 0
    return pl.pallas_call(
        _attn_kernel,
        grid=(b, N_HEADS // ATTN_HEADS, s // t),
        in_specs=[pl.BlockSpec((1, t, ATTN_HEADS * SLOT), lambda bi, p, i: (bi, i, p)),
                  pl.BlockSpec((1, s, ATTN_HEADS * SLOT), lambda bi, p, i: (bi, 0, p)),
                  pl.BlockSpec((1, s, ATTN_HEADS * SLOT), lambda bi, p, i: (bi, 0, p))],
        out_specs=pl.BlockSpec((1, t, ATTN_HEADS * B_V_DIM), lambda bi, p, i: (bi, i, p)),
        out_shape=jax.ShapeDtypeStruct((b, s, V_WIDTH), BF16),
        scratch_shapes=[pltpu.VMEM((ATTN_HEADS, t, 1), F32), pltpu.VMEM((ATTN_HEADS, t, LANES), F32)],
        compiler_params=_cparams(3),
        name="attention",
    )(q, k, v)


def _merge_kernel(x_ref, an_ref, ya_ref, yb_ref, wg_ref, wa_ref, wb_ref, wo_ref, o_ref):
    xf = x_ref[...]
    hn = _rms_rows(xf, an_ref[...]).astype(BF16)
    gates = _dot(hn, wg_ref[...])
    pa = _dot(ya_ref[...], wa_ref[...])
    pb = _dot(yb_ref[...], wb_ref[...])
    merged = _sigmoid(gates[:, :D_MODEL]) * pa + _sigmoid(gates[:, D_MODEL:]) * pb
    o_ref[...] = xf + _dot(merged.astype(BF16), wo_ref[...])


def _merge(x2, p, ya2, yb2, tm=512):
    t = x2.shape[0]
    full = lambda a: pl.BlockSpec(a.shape, lambda i: (0,) * a.ndim)
    tok = lambda w: pl.BlockSpec((tm, w), lambda i: (i, 0))
    consts = [p["w_gates"], p["w_br_a"], p["w_br_b"], p["w_out"]]
    return pl.pallas_call(
        _merge_kernel,
        grid=(t // tm,),
        in_specs=[tok(D_MODEL), full(p["attn_norm"]), tok(V_WIDTH), tok(V_WIDTH)] + [full(a) for a in consts],
        out_specs=tok(D_MODEL),
        out_shape=jax.ShapeDtypeStruct((t, D_MODEL), F32),
        compiler_params=_cparams(1),
        name="merge",
    )(x2, p["attn_norm"], ya2, yb2, *consts)


FF_CHUNK = MXU_DIM


def _ffn_kernel(x_ref, fn_ref, wgu_ref, wd_ref, o_ref, act_scr):
    xf = x_ref[...]
    h = _rms_rows(xf, fn_ref[...]).astype(BF16)
    for c in range(D_FF // FF_CHUNK):
        g = _dot(h, wgu_ref[:, c * FF_CHUNK:(c + 1) * FF_CHUNK])
        u = _dot(h, wgu_ref[:, D_FF + c * FF_CHUNK:D_FF + (c + 1) * FF_CHUNK])
        act_scr[:, c * FF_CHUNK:(c + 1) * FF_CHUNK] = (g * _sigmoid(g) * u).astype(BF16)
    o_ref[...] = xf + _dot(act_scr[...], wd_ref[...])


def _ffn_dense(x2, fn, w_gu, w_down, tm=512):
    t = x2.shape[0]
    full = lambda a: pl.BlockSpec(a.shape, lambda i: (0,) * a.ndim)
    tok = pl.BlockSpec((tm, D_MODEL), lambda i: (i, 0))
    return pl.pallas_call(
        _ffn_kernel,
        grid=(t // tm,),
        in_specs=[tok, full(fn), full(w_gu), full(w_down)],
        out_specs=tok,
        out_shape=jax.ShapeDtypeStruct((t, D_MODEL), F32),
        scratch_shapes=[pltpu.VMEM((tm, D_FF), BF16)],
        compiler_params=_cparams(1),
        name="ffn_dense",
    )(x2, fn, w_gu, w_down)


MOE_TILE = 512
MOE_CHUNK = 512
RUN_ALIGN = 8
MOE_ROWS = -(-(2 * MOE_CHUNK + N_EXPERTS * RUN_ALIGN) // LANES) * LANES
RUN_BITS = range(RUN_ALIGN.bit_length() - 1, (2 * MOE_CHUNK).bit_length())
PACKED = D_MODEL // 2
R_POS1, R_POS2, R_P1, R_P2 = range(4)
S_CNT, S_LOCAL, S_GLOBAL = range(3)


def _pack_bf16_pairs(xf):
    bits = lax.bitcast_convert_type(xf.astype(BF16).astype(F32), jnp.uint32)
    half = xf.shape[1] // 2
    return bits[:, :half] | (bits[:, half:] >> 16)


def _unpack_bf16_pairs(w):
    hi = lax.bitcast_convert_type(w & jnp.uint32(0xFFFF0000), F32)
    lo = lax.bitcast_convert_type(w << 16, F32)
    return jnp.concatenate([hi, lo], axis=1)


def _run_rows(first, size):
    return pl.ds(pl.multiple_of(first, RUN_ALIGN), size)


def _run_copies(table, src_of, dst_of, sem, start):
    for e in range(N_EXPERTS):
        n = table[S_CNT, e]
        for bit in RUN_BITS:
            size = 1 << bit
            above = n & ~(2 * size - 1)

            @pl.when((n & size) != 0)
            def _(e=e, size=size, above=above):
                cp = pltpu.make_async_copy(src_of(e, above, size), dst_of(e, above, size), sem)
                if start:
                    cp.start(priority=e % 2)
                else:
                    cp.wait()


def _router_kernel(x_ref, fn_ref, rw_ref, route_ref, cnt_ref, xs_ref,
                   carry_scr, stage, tab_v, tab_s, zeros_v, sem, tab_sem):
    c = pl.program_id(0)
    last = pl.num_programs(0) - 1
    tm = x_ref.shape[0]
    slot = c % 2
    cap = (xs_ref.shape[0] - MOE_TILE) // N_EXPERTS

    @pl.when(c == 0)
    def _():
        carry_scr[...] = jnp.zeros_like(carry_scr)

    hf = _rms_rows(x_ref[...], fn_ref[...])
    lane = lax.broadcasted_iota(jnp.int32, (tm, LANES), 1)
    logits = jnp.where(lane < N_EXPERTS, _dot3(hf, rw_ref[...]), -jnp.inf)
    m1 = jnp.max(logits, axis=1, keepdims=True)
    i1 = jnp.min(jnp.where(logits == m1, lane, LANES), axis=1, keepdims=True)
    rest = jnp.where(lane == i1, -jnp.inf, logits)
    m2 = jnp.max(rest, axis=1, keepdims=True)
    i2 = jnp.min(jnp.where(rest == m2, lane, LANES), axis=1, keepdims=True)
    e2 = jnp.exp(m2 - m1)
    p1 = 1.0 / (1.0 + e2)
    p2 = e2 / (1.0 + e2)

    onehot = jnp.where((lane == i1) | (lane == i2), 1.0, 0.0)
    r = lax.broadcasted_iota(jnp.int32, (tm, tm), 0)
    cc = lax.broadcasted_iota(jnp.int32, (tm, tm), 1)
    earlier = jnp.where(cc < r, 1.0, 0.0).astype(BF16)
    before = _dot(earlier, onehot.astype(BF16))
    er = lax.broadcasted_iota(jnp.int32, (LANES, LANES), 0)
    ec = lax.broadcasted_iota(jnp.int32, (LANES, LANES), 1)
    cnt = jnp.floor((jnp.sum(onehot, axis=0, keepdims=True) + (RUN_ALIGN - 1)) * (1.0 / RUN_ALIGN)) * RUN_ALIGN
    local = _dot(jnp.broadcast_to(cnt, (8, LANES)).astype(BF16), jnp.where(er < ec, 1.0, 0.0).astype(BF16))[0:1]
    pos = before + local
    pos1 = jnp.sum(jnp.where(lane == i1, pos, 0.0), axis=1, keepdims=True)
    pos2 = jnp.sum(jnp.where(lane == i2, pos, 0.0), axis=1, keepdims=True)
    rec = jnp.zeros((tm, LANES), F32)
    for where_, val in ((R_POS1, pos1), (R_POS2, pos2), (R_P1, p1), (R_P2, p2)):
        rec = jnp.where(lane == where_, val, rec)
    route_ref[...] = rec

    slot_lane = lax.broadcasted_iota(jnp.int32, (tm, MOE_ROWS), 1).astype(F32)
    pick_t = jnp.where((slot_lane == pos1) | (slot_lane == pos2), 1.0, 0.0)
    sorted_rows = _dot(pick_t.T.astype(BF16), hf.astype(BF16))

    stage[slot] = _pack_bf16_pairs(sorted_rows)

    carry = carry_scr[0:1, :]
    srow = lax.broadcasted_iota(jnp.int32, (8, LANES), 0)
    elane = lax.broadcasted_iota(jnp.int32, (8, LANES), 1)
    table = jnp.where(srow == S_CNT, cnt, jnp.where(srow == S_LOCAL, local, carry + (elane * cap).astype(F32)))
    tab_v[...] = table.astype(jnp.int32)
    total = carry + cnt
    carry_scr[...] = jnp.broadcast_to(total, carry_scr.shape)
    cnt_ref[0] = jnp.broadcast_to(cnt, (8, LANES))

    tcp = pltpu.make_async_copy(tab_v, tab_s.at[slot], tab_sem)
    tcp.start()
    tcp.wait()

    def runs(s, start):
        _run_copies(tab_s.at[s],
                    lambda e, off, size: stage.at[s, _run_rows(tab_s[s, S_LOCAL, e] + off, size)],
                    lambda e, off, size: xs_ref.at[_run_rows(tab_s[s, S_GLOBAL, e] + off, size)],
                    sem.at[s], start)

    runs(slot, True)

    @pl.when(c > 0)
    def _():
        runs(1 - slot, False)

    @pl.when(c == last)
    def _():
        runs(slot, False)
        zeros_v[...] = jnp.zeros_like(zeros_v)
        pad = (-total.astype(jnp.int32)) & (MOE_TILE - 1)
        tab_v[...] = jnp.where(srow == S_CNT, pad, jnp.where(srow == S_LOCAL, 0,
                                                               total.astype(jnp.int32) + elane * cap))
        pcp = pltpu.make_async_copy(tab_v, tab_s.at[slot], tab_sem)
        pcp.start()
        pcp.wait()
        for start in (True, False):
            _run_copies(tab_s.at[slot],
                        lambda e, off, size: zeros_v.at[_run_rows(off, size)],
                        lambda e, off, size: xs_ref.at[_run_rows(tab_s[slot, S_GLOBAL, e] + off, size)],
                        sem.at[slot], start)


def _expert_region(t):
    worst = t + (t // MOE_CHUNK) * (RUN_ALIGN - 1)
    return -(-worst // MOE_TILE) * MOE_TILE


def _router(x2, fn, router_w):
    t = x2.shape[0]
    tm = MOE_CHUNK
    nc = t // tm
    n_rows = N_EXPERTS * _expert_region(t) + MOE_TILE
    full = lambda a: pl.BlockSpec(a.shape, lambda i: (0,) * a.ndim)
    return pl.pallas_call(
        _router_kernel,
        grid=(nc,),
        in_specs=[pl.BlockSpec((tm, D_MODEL), lambda i: (i, 0)), full(fn), full(router_w)],
        out_specs=[pl.BlockSpec((tm, LANES), lambda i: (i, 0)), pl.BlockSpec((1, 8, LANES), lambda i: (i, 0, 0)),
                   pl.BlockSpec(memory_space=pl.ANY)],
        out_shape=[jax.ShapeDtypeStruct((t, LANES), F32), jax.ShapeDtypeStruct((nc, 8, LANES), F32),
                   jax.ShapeDtypeStruct((n_rows, PACKED), jnp.uint32)],
        scratch_shapes=[pltpu.VMEM((8, LANES), F32), pltpu.VMEM((2, MOE_ROWS, PACKED), jnp.uint32),
                        pltpu.VMEM((8, LANES), jnp.int32), pltpu.SMEM((2, 8, LANES), jnp.int32),
                        pltpu.VMEM((MOE_TILE, PACKED), jnp.uint32),
                        pltpu.SemaphoreType.DMA((2,)), pltpu.SemaphoreType.DMA(())],
        compiler_params=_cparams(1),
        name="moe_router",
    )(x2, fn, router_w)


def _experts_kernel(tb_ref, te_ref, tv_ref, xs_ref, wgu_ref, wd_ref, y_ref, act_scr):
    del tb_ref, te_ref
    j = pl.program_id(0)

    @pl.when(tv_ref[j] != 0)
    def _():
        h = _unpack_bf16_pairs(xs_ref[...]).astype(BF16)
        for c in range(D_FF // FF_CHUNK):
            g = _dot(h, wgu_ref[0, :, c * FF_CHUNK:(c + 1) * FF_CHUNK])
            u = _dot(h, wgu_ref[0, :, D_FF + c * FF_CHUNK:D_FF + (c + 1) * FF_CHUNK])
            act_scr[:, c * FF_CHUNK:(c + 1) * FF_CHUNK] = (g * _sigmoid(g) * u).astype(BF16)
        y_ref[...] = _pack_bf16_pairs(_dot(act_scr[...], wd_ref[0]))

    @pl.when(tv_ref[j] == 0)
    def _():
        y_ref[...] = jnp.zeros_like(y_ref)


def _experts(tile_block, tile_expert, tile_valid, xs, w_gu, w_down):
    tm = MOE_TILE
    row = pl.BlockSpec((tm, PACKED), lambda j, tb, te, tv: (tb[j], 0))
    return pl.pallas_call(
        _experts_kernel,
        grid_spec=pltpu.PrefetchScalarGridSpec(
            num_scalar_prefetch=3,
            grid=(tile_block.shape[0],),
            in_specs=[row,
                      pl.BlockSpec((1, D_MODEL, 2 * D_FF), lambda j, tb, te, tv: (te[j], 0, 0)),
                      pl.BlockSpec((1, D_FF, D_MODEL), lambda j, tb, te, tv: (te[j], 0, 0))],
            out_specs=row,
            scratch_shapes=[pltpu.VMEM((tm, D_FF), BF16)]),
        out_shape=jax.ShapeDtypeStruct(xs.shape, jnp.uint32),
        compiler_params=_cparams(1),
        name="moe_experts",
    )(tile_block, tile_expert, tile_valid, xs, w_gu, w_down)


def _combine_kernel(tcur_ref, tnext_ref, x_ref, route_ref, y_ref, o_ref, ybuf, sem):
    c = pl.program_id(0)
    last = pl.num_programs(0) - 1
    tm = x_ref.shape[0]
    slot = c % 2

    def runs(table, s, start):
        _run_copies(table.at[0],
                    lambda e, off, size: y_ref.at[_run_rows(table[0, S_GLOBAL, e] + off, size)],
                    lambda e, off, size: ybuf.at[s, _run_rows(table[0, S_LOCAL, e] + off, size)],
                    sem.at[s], start)

    @pl.when(c == 0)
    def _():
        ybuf[...] = jnp.zeros_like(ybuf)
        runs(tcur_ref, 0, True)

    @pl.when(c < last)
    def _():
        runs(tnext_ref, 1 - slot, True)

    runs(tcur_ref, slot, False)

    rec = route_ref[...]
    ys = _unpack_bf16_pairs(ybuf[slot]).astype(BF16)
    slot_lane = lax.broadcasted_iota(jnp.int32, (tm, MOE_ROWS), 1).astype(F32)
    out = x_ref[...]
    for pos_lane, p_lane in ((R_POS1, R_P1), (R_POS2, R_P2)):
        pick = jnp.where(slot_lane == rec[:, pos_lane:pos_lane + 1], 1.0, 0.0).astype(BF16)
        out = out + rec[:, p_lane:p_lane + 1] * _dot(pick, ys)
    o_ref[...] = out


def _combine(tables, x2, route, y):
    t = x2.shape[0]
    tm = MOE_CHUNK
    nc = t // tm
    tok = lambda w: pl.BlockSpec((tm, w), lambda c: (c, 0))
    tab = lambda f: pl.BlockSpec((1, 3, N_EXPERTS), lambda c: (f(c), 0, 0), memory_space=pltpu.SMEM)
    return pl.pallas_call(
        _combine_kernel,
        grid=(nc,),
        in_specs=[tab(lambda c: c), tab(lambda c: jnp.minimum(c + 1, nc - 1)),
                  tok(D_MODEL), tok(LANES), pl.BlockSpec(memory_space=pl.ANY)],
        out_specs=tok(D_MODEL),
        out_shape=jax.ShapeDtypeStruct((t, D_MODEL), F32),
        scratch_shapes=[pltpu.VMEM((2, MOE_ROWS, PACKED), jnp.uint32), pltpu.SemaphoreType.DMA((2,))],
        compiler_params=_cparams(1),
        name="moe_combine",
    )(tables, tables, x2, route, y)


def _ffn_moe(x2, fn, router_w, w_gu, w_down):
    t = x2.shape[0]
    route, cnt, xs = _router(x2, fn, router_w)
    counts = cnt[:, 0, :N_EXPERTS].astype(jnp.int32)
    upto = jnp.cumsum(counts, axis=0)
    experts = jnp.arange(N_EXPERTS, dtype=jnp.int32)
    region = _expert_region(t)
    tables = jnp.stack([counts, jnp.cumsum(counts, axis=1) - counts,
                        upto - counts + experts[None, :] * region], axis=1)
    tiles = (upto[-1] + MOE_TILE - 1) // MOE_TILE
    tile_end = jnp.cumsum(tiles)
    max_rows = 2 * t + (t // MOE_CHUNK) * N_EXPERTS * (RUN_ALIGN - 1)
    n_steps = max_rows // MOE_TILE + N_EXPERTS
    j = jnp.arange(n_steps, dtype=jnp.int32)
    tile_expert = jnp.minimum(jnp.sum((j[:, None] >= tile_end[None, :]).astype(jnp.int32), axis=1), N_EXPERTS - 1)
    first_tile = jnp.sum(jnp.where(experts[None, :] == tile_expert[:, None], (tile_end - tiles)[None, :], 0), axis=1)
    tile_valid = (j < tile_end[-1]).astype(jnp.int32)
    spare = N_EXPERTS * region // MOE_TILE
    tile_block = jnp.where(tile_valid != 0, tile_expert * (region // MOE_TILE) + (j - first_tile), spare)
    y = _experts(tile_block, tile_expert, tile_valid, xs, w_gu, w_down)
    return _combine(tables, x2, route, y)


def _slot_cols(w, dims):
    k = w.shape[0]
    w = w.reshape(k, N_HEADS, dims)
    return jnp.pad(w, ((0, 0), (0, 0), (0, SLOT - dims))).reshape(k, HW)


def _slot_vec(g, dims):
    return jnp.tile(jnp.pad(g, (0, SLOT - dims)), N_HEADS).reshape(1, HW)


def _rope_table(seq, rot_dim, lane0):
    half = rot_dim // 2
    inv = ROPE_THETA ** (-jnp.arange(0, rot_dim, 2, dtype=F32) / rot_dim)
    ang = jnp.arange(seq, dtype=F32)[:, None] * inv[None, :]
    cos, sin = jnp.cos(ang), jnp.sin(ang)
    c = jnp.ones((seq, SLOT), F32).at[:, lane0:lane0 + half].set(cos).at[:, lane0 + half:lane0 + rot_dim].set(cos)
    s_fwd = jnp.zeros((seq, SLOT), F32).at[:, lane0 + half:lane0 + rot_dim].set(sin)
    s_bwd = jnp.zeros((seq, SLOT), F32).at[:, lane0:lane0 + half].set(-sin)
    return jnp.stack([c, s_fwd, s_bwd])


def _constants():
    r = np.arange(MXU_DIM)
    seg = (r[:, None] // SLOT == r[None, :] // SLOT).astype(np.float32)
    place = np.zeros((N_HEADS * MAX_MOBA_BLOCKS, HW), np.float32)
    for h in range(N_HEADS):
        for n in range(MAX_MOBA_BLOCKS):
            place[h * MAX_MOBA_BLOCKS + n, h * SLOT + BIAS_LANE0 + n] = 1.0
    return jnp.asarray(seg, BF16), jnp.asarray(place, BF16)


def _layer_params(l, attn_norm, w_in, moba_q_norm, moba_k_norm, mla_cq_norm, w_uq, mla_ckv_norm,
                  w_ukv, mla_q_norm, mla_k_norm, w_branch_a, w_branch_b, w_out):
    wi = w_in[l]
    aw = N_HEADS * A_HEAD_DIM
    o_cq = 3 * aw
    o_ckv = o_cq + B_Q_RANK
    o_kr = o_ckv + B_KV_RANK
    o_g = o_kr + B_ROPE_DIM
    w_proj = jnp.concatenate([
        _slot_cols(wi[:, 0:aw], A_HEAD_DIM), _slot_cols(wi[:, aw:2 * aw], A_HEAD_DIM),
        _slot_cols(wi[:, 2 * aw:3 * aw], A_HEAD_DIM),
        wi[:, o_cq:o_ckv], wi[:, o_ckv:o_kr], jnp.pad(wi[:, o_kr:o_g], ((0, 0), (0, LANES - B_ROPE_DIM)))], axis=1)
    ukv = w_ukv[l].reshape(B_KV_RANK, N_HEADS, B_NOPE_DIM + B_V_DIM)
    w_ukv_s = jnp.concatenate([_slot_cols(ukv[:, :, :B_NOPE_DIM].reshape(B_KV_RANK, -1), B_NOPE_DIM),
                               _slot_cols(ukv[:, :, B_NOPE_DIM:].reshape(B_KV_RANK, -1), B_V_DIM)], axis=1)
    seg, place = _constants()
    return {
        "attn_norm": attn_norm[l].reshape(1, D_MODEL),
        "w_proj": w_proj.astype(BF16),
        "w_gates": wi[:, o_g:].astype(BF16),
        "w_uq": _slot_cols(w_uq[l], B_QK_DIM).astype(BF16),
        "w_ukv": w_ukv_s.astype(BF16),
        "cq_norm": mla_cq_norm[l].reshape(1, B_Q_RANK),
        "ckv_norm": mla_ckv_norm[l].reshape(1, B_KV_RANK),
        "qn_a": _slot_vec(moba_q_norm[l], A_HEAD_DIM),
        "kn_a": _slot_vec(moba_k_norm[l], A_HEAD_DIM),
        "qn_b": _slot_vec(mla_q_norm[l], B_QK_DIM),
        "kn_b": _slot_vec(mla_k_norm[l], B_QK_DIM),
        "seg": seg,
        "place": place,
        "w_br_a": w_branch_a[l].astype(BF16),
        "w_br_b": w_branch_b[l].astype(BF16),
        "w_out": w_out[l].astype(BF16),
    }


def kernel(x, attn_norm, w_in, moba_q_norm, moba_k_norm, mla_cq_norm, w_uq, mla_ckv_norm, w_ukv,
           mla_q_norm, mla_k_norm, w_branch_a, w_branch_b, w_out, ffn_norm, dense_w_gate_up,
           dense_w_down, router_w, expert_w_gate_up, expert_w_down):
    b, s, d = x.shape
    depth = attn_norm.shape[0]
    rope_a = _rope_table(s, A_ROT_DIM, 0)
    rope_b = _rope_table(s, B_ROPE_DIM, B_NOPE_DIM)
    for l in range(depth):
        p = _layer_params(l, attn_norm, w_in, moba_q_norm, moba_k_norm, mla_cq_norm, w_uq,
                          mla_ckv_norm, w_ukv, mla_q_norm, mla_k_norm, w_branch_a, w_branch_b, w_out)
        qa, ka, va, qb, kb, vb = _prologue(x, p, rope_a, rope_b)
        ya = _attention(qa, ka, va)
        yb = _attention(qb, kb, vb)
        x2 = _merge(x.reshape(b * s, d), p, ya.reshape(b * s, V_WIDTH), yb.reshape(b * s, V_WIDTH))
        fn = ffn_norm[l].reshape(1, D_MODEL)
        if l % 2 == 0:
            x2 = _ffn_dense(x2, fn, dense_w_gate_up[l // 2].astype(BF16), dense_w_down[l // 2].astype(BF16))
        else:
            rw = jnp.pad(router_w[l // 2], ((0, 0), (0, LANES - N_EXPERTS)))
            x2 = _ffn_moe(x2, fn, rw, expert_w_gate_up[l // 2].astype(BF16), expert_w_down[l // 2].astype(BF16))
        x = x2.reshape(b, s, d)
    return x
```

```python
import math

import jax
import jax.numpy as jnp
import numpy as np
from jax import lax
from jax.experimental import pallas as pl
from jax.experimental.pallas import tpu as pltpu

F32 = jnp.float32
BF16 = jnp.bfloat16

D_MODEL = 1024
N_HEADS = 8
A_HEAD_DIM = 64
A_ROT_DIM = 16
MOBA_BLOCK = 256
MOBA_TOPK = 3
MAX_MOBA_BLOCKS = 16
B_NOPE_DIM = 64
B_ROPE_DIM = 32
B_QK_DIM = B_NOPE_DIM + B_ROPE_DIM
B_V_DIM = 64
B_Q_RANK = 384
B_KV_RANK = 256
ROPE_THETA = 500000.0
EPS = 1e-6
NEG = -1e30
D_FF = 2816
N_EXPERTS = 8

LANES = 128
SLOT = LANES
MXU_DIM = 256
HW = N_HEADS * SLOT
BIAS_LANE0 = A_HEAD_DIM
V_WIDTH = N_HEADS * B_V_DIM

C_QA, C_KA, C_VA = 0, HW, 2 * HW
C_CQ = C_VA + HW
C_CKV = C_CQ + B_Q_RANK
C_KR = C_CKV + B_KV_RANK
N_PROJ = C_KR + LANES
SUM_LANE = B_V_DIM
LOG2E = math.log2(math.e)

VMEM_LIMIT = 56 * 1024 * 1024


def _cparams(n_axes):
    return pltpu.CompilerParams(dimension_semantics=("arbitrary",) * n_axes,
                                vmem_limit_bytes=VMEM_LIMIT)


def _dot(a, b):
    return jnp.dot(a, b, preferred_element_type=F32)


def _dot_nt(a, b):
    return lax.dot_general(a, b, (((1,), (1,)), ((), ())), preferred_element_type=F32)


def _split_bf16(a):
    hi = a.astype(BF16)
    lo = (a - hi.astype(F32)).astype(BF16)
    return hi, lo


def _dot3(a, b):
    ah, al = _split_bf16(a)
    bh, bl = _split_bf16(b)
    return _dot(ah, bh) + (_dot(ah, bl) + _dot(al, bh))


def _dot3_nt(a, b):
    ah, al = _split_bf16(a)
    bh, bl = _split_bf16(b)
    return _dot_nt(ah, bh) + (_dot_nt(ah, bl) + _dot_nt(al, bh))


def _rms_rows(xf, g):
    ms = jnp.mean(xf * xf, axis=-1, keepdims=True)
    return xf * lax.rsqrt(ms + EPS) * g


def _sigmoid(z):
    return 1.0 / (1.0 + jnp.exp(-z))


PROLOGUE_BLOCKS = 2


def _slot_norm_rope(x, g_ref, seg_ref, inv_n, rope, shift):
    cos, sin_fwd, sin_bwd = rope
    slots = []
    for c in range(HW // MXU_DIM):
        xc = x[:, c * MXU_DIM:(c + 1) * MXU_DIM]
        ss = _dot((xc * xc).astype(BF16), seg_ref[...])
        xn = xc * lax.rsqrt(ss * inv_n + EPS) * g_ref[:, c * MXU_DIM:(c + 1) * MXU_DIM]
        for j in range(MXU_DIM // SLOT):
            xs = xn[:, j * SLOT:(j + 1) * SLOT]
            slots.append(xs * cos + pltpu.roll(xs, shift, 1) * sin_fwd
                         + pltpu.roll(xs, SLOT - shift, 1) * sin_bwd)
    return slots


def _prologue_kernel(x_ref, an_ref, w_ref, wuq_ref, wukv_ref, cqn_ref, ckvn_ref,
                     qna_ref, kna_ref, qnb_ref, knb_ref, seg_ref, place_ref,
                     ropea_ref, ropeb_ref,
                     qa_ref, ka_ref, va_ref, qb_ref, kb_ref, vb_ref, kmean_scr):
    @pl.when(pl.program_id(1) == 0)
    def _():
        kmean_scr[...] = jnp.zeros_like(kmean_scr)

    out_refs = (qa_ref, ka_ref, va_ref, qb_ref, kb_ref, vb_ref)
    for j in range(x_ref.shape[1] // MOBA_BLOCK):
        rows = slice(j * MOBA_BLOCK, (j + 1) * MOBA_BLOCK)
        hn = _rms_rows(x_ref[0, rows], an_ref[...]).astype(BF16)
        proj = _dot(hn, w_ref[...])
        _prologue_block(proj, pl.program_id(1) * (x_ref.shape[1] // MOBA_BLOCK) + j, rows,
                        wuq_ref, wukv_ref, cqn_ref, ckvn_ref, qna_ref, kna_ref, qnb_ref, knb_ref,
                        seg_ref, place_ref, [ropea_ref[t, rows] for t in range(3)],
                        [ropeb_ref[t, rows] for t in range(3)], out_refs, kmean_scr)


def _prologue_block(proj, s, rows, wuq_ref, wukv_ref, cqn_ref, ckvn_ref, qna_ref, kna_ref, qnb_ref, knb_ref,
                    seg_ref, place_ref, rope_a, rope_b, out_refs, kmean_scr):
    qa_ref, ka_ref, va_ref, qb_ref, kb_ref, vb_ref = out_refs
    tm = MOBA_BLOCK

    qa = _slot_norm_rope(proj[:, C_QA:C_QA + HW], qna_ref, seg_ref, 1.0 / A_HEAD_DIM, rope_a, A_ROT_DIM // 2)
    ka = _slot_norm_rope(proj[:, C_KA:C_KA + HW], kna_ref, seg_ref, 1.0 / A_HEAD_DIM, rope_a, A_ROT_DIM // 2)

    kmean = kmean_scr[...]
    gates = [_dot3_nt(kmean[:, h * SLOT:(h + 1) * SLOT], qa[h]) for h in range(N_HEADS)]
    g3 = jnp.stack(gates, axis=0)
    blk = lax.broadcasted_iota(jnp.int32, g3.shape, 1)
    g3 = jnp.where(blk < s, g3, -jnp.inf)
    rank3 = jnp.zeros(g3.shape, F32)
    for n in range(MAX_MOBA_BLOCKS):
        row = g3[:, n:n + 1, :]
        rank3 = rank3 + jnp.where((row > g3) | ((row == g3) & (blk > n)), 1.0, 0.0)
    sel3 = jnp.where(((rank3 < MOBA_TOPK) & (blk < s)) | (blk == s), 1.0, 0.0)
    sel_t = sel3.reshape(N_HEADS * MAX_MOBA_BLOCKS, tm)
    placed = _dot(sel_t.T.astype(BF16), place_ref[...])
    lane = lax.broadcasted_iota(jnp.int32, (tm, SLOT), 1)
    is_bias = (lane >= BIAS_LANE0) & (lane < BIAS_LANE0 + MAX_MOBA_BLOCKS)
    scale_a = LOG2E / math.sqrt(A_HEAD_DIM)
    for h in range(N_HEADS):
        sl = slice(h * SLOT, (h + 1) * SLOT)
        bias = (placed[:, sl] - 1.0) * (-NEG)
        qa_ref[0, rows, sl] = jnp.where(is_bias, bias, qa[h] * scale_a).astype(BF16)
        ka_ref[0, rows, sl] = jnp.where(lane == BIAS_LANE0 + s, 1.0, ka[h]).astype(BF16)
        va_ref[0, rows, sl] = jnp.where(lane == SUM_LANE, 1.0,
                                        proj[:, C_VA + h * SLOT:C_VA + (h + 1) * SLOT]).astype(BF16)

    km = jnp.concatenate([jnp.mean(k, axis=0, keepdims=True) for k in ka], axis=1)
    kmean_scr[pl.ds(s, 1), :] = km

    cq = _rms_rows(proj[:, C_CQ:C_CQ + B_Q_RANK], cqn_ref[...]).astype(BF16)
    qb = _dot(cq, wuq_ref[...])
    ckv = _rms_rows(proj[:, C_CKV:C_CKV + B_KV_RANK], ckvn_ref[...]).astype(BF16)
    kv = _dot(ckv, wukv_ref[...])
    kr = pltpu.roll(proj[:, C_KR:C_KR + LANES], B_NOPE_DIM, 1)
    kb = kv[:, :HW] + jnp.concatenate([kr] * N_HEADS, axis=1)
    scale_b = LOG2E / math.sqrt(B_QK_DIM)
    qb_s = _slot_norm_rope(qb, qnb_ref, seg_ref, 1.0 / B_QK_DIM, rope_b, B_ROPE_DIM // 2)
    kb_s = _slot_norm_rope(kb, knb_ref, seg_ref, 1.0 / B_QK_DIM, rope_b, B_ROPE_DIM // 2)
    for h in range(N_HEADS):
        sl = slice(h * SLOT, (h + 1) * SLOT)
        qb_ref[0, rows, sl] = (qb_s[h] * scale_b).astype(BF16)
        kb_ref[0, rows, sl] = kb_s[h].astype(BF16)
        vb_ref[0, rows, sl] = jnp.where(lane == SUM_LANE, 1.0,
                                        kv[:, HW + h * SLOT:HW + (h + 1) * SLOT]).astype(BF16)


def _prologue(x, p, rope_a, rope_b):
    b, s, _ = x.shape
    tm = PROLOGUE_BLOCKS * MOBA_BLOCK
    assert s % tm == 0 and s // MOBA_BLOCK <= MAX_MOBA_BLOCKS
    full = lambda a: pl.BlockSpec(a.shape, lambda i, j: (0,) * a.ndim)
    tok = lambda w: pl.BlockSpec((1, tm, w), lambda i, j: (i, j, 0))
    rope = pl.BlockSpec((3, tm, LANES), lambda i, j: (0, j, 0))
    consts = [p["attn_norm"], p["w_proj"], p["w_uq"], p["w_ukv"], p["cq_norm"], p["ckv_norm"],
              p["qn_a"], p["kn_a"], p["qn_b"], p["kn_b"], p["seg"], p["place"]]
    out_shape = [jax.ShapeDtypeStruct((b, s, HW), BF16)] * 6
    return pl.pallas_call(
        _prologue_kernel,
        grid=(b, s // tm),
        in_specs=[tok(D_MODEL)] + [full(a) for a in consts] + [rope, rope],
        out_specs=[tok(HW)] * 6,
        out_shape=out_shape,
        scratch_shapes=[pltpu.VMEM((MAX_MOBA_BLOCKS, HW), F32)],
        compiler_params=_cparams(2),
        name="prologue",
    )(x, *consts, rope_a, rope_b)


ATTN_TILE = 1024
ATTN_HEADS = 4
DIAG_SPLIT = 2


def _attn_kernel(q_ref, k_ref, v_ref, o_ref, m_scr, acc_scr):
    i = pl.program_id(2)
    t = q_ref.shape[1]
    tr = t // DIAG_SPLIT

    def scores(n, hh):
        sl = slice(hh * SLOT, (hh + 1) * SLOT)
        off = pl.multiple_of(n * t, t)
        return _dot_nt(q_ref[0, :, sl], k_ref[0, pl.ds(off, t), sl]), v_ref[0, pl.ds(off, t), sl]

    diag = pl.multiple_of(i * t, t)
    for hh in range(ATTN_HEADS):
        sl = slice(hh * SLOT, (hh + 1) * SLOT)
        for r in range(DIAG_SPLIT):
            rows = slice(r * tr, (r + 1) * tr)
            nk = (r + 1) * tr
            sc = _dot_nt(q_ref[0, rows, sl], k_ref[0, pl.ds(diag, nk), sl])
            row = lax.broadcasted_iota(jnp.int32, (tr, nk), 0)
            col = lax.broadcasted_iota(jnp.int32, (tr, nk), 1)
            sc = jnp.where(col <= row + r * tr, sc, NEG)
            m = jnp.max(sc, axis=1, keepdims=True)
            m_scr[hh, rows] = m
            acc_scr[hh, rows] = _dot(jnp.exp2(sc - m).astype(BF16), v_ref[0, pl.ds(diag, nk), sl])
    first = [(jnp.max(jnp.broadcast_to(m_scr[hh], (t, LANES)), axis=1, keepdims=True), acc_scr[hh])
             for hh in range(ATTN_HEADS)]

    def past_tile(n, state):
        new = []
        for hh, (m, acc) in enumerate(state):
            sc, v = scores(n, hh)
            m_new = jnp.maximum(m, jnp.max(sc, axis=1, keepdims=True))
            new.append((m_new, jnp.exp2(m - m_new) * acc + _dot(jnp.exp2(sc - m_new).astype(BF16), v)))
        return tuple(new)

    last = lax.fori_loop(0, i, past_tile, tuple(first))
    outs = [acc / acc[:, SUM_LANE:SUM_LANE + 1] for _, acc in last]
    lane = lax.broadcasted_iota(jnp.int32, (t, LANES), 1)
    for p in range(ATTN_HEADS // 2):
        pair = jnp.where(lane < B_V_DIM, outs[2 * p], pltpu.roll(outs[2 * p + 1], B_V_DIM, 1))
        o_ref[0, :, p * LANES:(p + 1) * LANES] = pair.astype(o_ref.dtype)


def _attention(q, k, v):
    b, s, _ = q.shape
    t = min(ATTN_TILE, s)
    assert s % t == 0 and t % MOBA_BLOCK == 0
    return pl.pallas_call(
        _attn_kernel,
        grid=(b, N_HEADS // ATTN_HEADS, s // t),
        in_specs=[pl.BlockSpec((1, t, ATTN_HEADS * SLOT), lambda bi, p, i: (bi, i, p)),
                  pl.BlockSpec((1, s, ATTN_HEADS * SLOT), lambda bi, p, i: (bi, 0, p)),
                  pl.BlockSpec((1, s, ATTN_HEADS * SLOT), lambda bi, p, i: (bi, 0, p))],
        out_specs=pl.BlockSpec((1, t, ATTN_HEADS * B_V_DIM), lambda bi, p, i: (bi, i, p)),
        out_shape=jax.ShapeDtypeStruct((b, s, V_WIDTH), BF16),
        scratch_shapes=[pltpu.VMEM((ATTN_HEADS, t, 1), F32), pltpu.VMEM((ATTN_HEADS, t, LANES), F32)],
        compiler_params=_cparams(3),
        name="attention",
    )(q, k, v)


def _merge_kernel(x_ref, an_ref, ya_ref, yb_ref, wg_ref, wa_ref, wb_ref, wo_ref, o_ref):
    xf = x_ref[...]
    hn = _rms_rows(xf, an_ref[...]).astype(BF16)
    gates = _dot(hn, wg_ref[...])
    pa = _dot(ya_ref[...], wa_ref[...])
    pb = _dot(yb_ref[...], wb_ref[...])
    merged = _sigmoid(gates[:, :D_MODEL]) * pa + _sigmoid(gates[:, D_MODEL:]) * pb
    o_ref[...] = xf + _dot(merged.astype(BF16), wo_ref[...])


def _merge(x2, p, ya2, yb2, tm=512):
    t = x2.shape[0]
    full = lambda a: pl.BlockSpec(a.shape, lambda i: (0,) * a.ndim)
    tok = lambda w: pl.BlockSpec((tm, w), lambda i: (i, 0))
    consts = [p["w_gates"], p["w_br_a"], p["w_br_b"], p["w_out"]]
    return pl.pallas_call(
        _merge_kernel,
        grid=(t // tm,),
        in_specs=[tok(D_MODEL), full(p["attn_norm"]), tok(V_WIDTH), tok(V_WIDTH)] + [full(a) for a in consts],
        out_specs=tok(D_MODEL),
        out_shape=jax.ShapeDtypeStruct((t, D_MODEL), F32),
        compiler_params=_cparams(1),
        name="merge",
    )(x2, p["attn_norm"], ya2, yb2, *consts)


FF_CHUNK = MXU_DIM


def _ffn_kernel(x_ref, fn_ref, wgu_ref, wd_ref, o_ref, act_scr):
    xf = x_ref[...]
    h = _rms_rows(xf, fn_ref[...]).astype(BF16)
    for c in range(D_FF // FF_CHUNK):
        g = _dot(h, wgu_ref[:, c * FF_CHUNK:(c + 1) * FF_CHUNK])
        u = _dot(h, wgu_ref[:, D_FF + c * FF_CHUNK:D_FF + (c + 1) * FF_CHUNK])
        act_scr[:, c * FF_CHUNK:(c + 1) * FF_CHUNK] = (g * _sigmoid(g) * u).astype(BF16)
    o_ref[...] = xf + _dot(act_scr[...], wd_ref[...])


def _ffn_dense(x2, fn, w_gu, w_down, tm=512):
    t = x2.shape[0]
    full = lambda a: pl.BlockSpec(a.shape, lambda i: (0,) * a.ndim)
    tok = pl.BlockSpec((tm, D_MODEL), lambda i: (i, 0))
    return pl.pallas_call(
        _ffn_kernel,
        grid=(t // tm,),
        in_specs=[tok, full(fn), full(w_gu), full(w_down)],
        out_specs=tok,
        out_shape=jax.ShapeDtypeStruct((t, D_MODEL), F32),
        scratch_shapes=[pltpu.VMEM((tm, D_FF), BF16)],
        compiler_params=_cparams(1),
        name="ffn_dense",
    )(x2, fn, w_gu, w_down)


MOE_TILE = 512
MOE_CHUNK = 512
RUN_ALIGN = 8
MOE_ROWS = -(-(2 * MOE_CHUNK + N_EXPERTS * RUN_ALIGN) // LANES) * LANES
RUN_BITS = range(RUN_ALIGN.bit_length() - 1, (2 * MOE_CHUNK).bit_length())
PACKED = D_MODEL // 2
R_POS1, R_POS2, R_P1, R_P2 = range(4)
S_CNT, S_LOCAL, S_GLOBAL = range(3)


def _pack_bf16_pairs(xf):
    bits = lax.bitcast_convert_type(xf.astype(BF16).astype(F32), jnp.uint32)
    half = xf.shape[1] // 2
    return bits[:, :half] | (bits[:, half:] >> 16)


def _unpack_bf16_pairs(w):
    hi = lax.bitcast_convert_type(w & jnp.uint32(0xFFFF0000), F32)
    lo = lax.bitcast_convert_type(w << 16, F32)
    return jnp.concatenate([hi, lo], axis=1)


def _run_rows(first, size):
    return pl.ds(pl.multiple_of(first, RUN_ALIGN), size)


def _run_copies(table, src_of, dst_of, sem, start):
    for e in range(N_EXPERTS):
        n = table[S_CNT, e]
        for bit in RUN_BITS:
            size = 1 << bit
            above = n & ~(2 * size - 1)

            @pl.when((n & size) != 0)
            def _(e=e, size=size, above=above):
                cp = pltpu.make_async_copy(src_of(e, above, size), dst_of(e, above, size), sem)
                if start:
                    cp.start(priority=e % 2)
                else:
                    cp.wait()


def _router_kernel(x_ref, fn_ref, rw_ref, route_ref, cnt_ref, xs_ref,
                   carry_scr, stage, tab_v, tab_s, zeros_v, sem, tab_sem):
    c = pl.program_id(0)
    last = pl.num_programs(0) - 1
    tm = x_ref.shape[0]
    slot = c % 2
    cap = (xs_ref.shape[0] - MOE_TILE) // N_EXPERTS

    @pl.when(c == 0)
    def _():
        carry_scr[...] = jnp.zeros_like(carry_scr)

    hf = _rms_rows(x_ref[...], fn_ref[...])
    lane = lax.broadcasted_iota(jnp.int32, (tm, LANES), 1)
    logits = jnp.where(lane < N_EXPERTS, _dot3(hf, rw_ref[...]), -jnp.inf)
    m1 = jnp.max(logits, axis=1, keepdims=True)
    i1 = jnp.min(jnp.where(logits == m1, lane, LANES), axis=1, keepdims=True)
    rest = jnp.where(lane == i1, -jnp.inf, logits)
    m2 = jnp.max(rest, axis=1, keepdims=True)
    i2 = jnp.min(jnp.where(rest == m2, lane, LANES), axis=1, keepdims=True)
    e2 = jnp.exp(m2 - m1)
    p1 = 1.0 / (1.0 + e2)
    p2 = e2 / (1.0 + e2)

    onehot = jnp.where((lane == i1) | (lane == i2), 1.0, 0.0)
    r = lax.broadcasted_iota(jnp.int32, (tm, tm), 0)
    cc = lax.broadcasted_iota(jnp.int32, (tm, tm), 1)
    earlier = jnp.where(cc < r, 1.0, 0.0).astype(BF16)
    before = _dot(earlier, onehot.astype(BF16))
    er = lax.broadcasted_iota(jnp.int32, (LANES, LANES), 0)
    ec = lax.broadcasted_iota(jnp.int32, (LANES, LANES), 1)
    cnt = jnp.floor((jnp.sum(onehot, axis=0, keepdims=True) + (RUN_ALIGN - 1)) * (1.0 / RUN_ALIGN)) * RUN_ALIGN
    local = _dot(jnp.broadcast_to(cnt, (8, LANES)).astype(BF16), jnp.where(er < ec, 1.0, 0.0).astype(BF16))[0:1]
    pos = before + local
    pos1 = jnp.sum(jnp.where(lane == i1, pos, 0.0), axis=1, keepdims=True)
    pos2 = jnp.sum(jnp.where(lane == i2, pos, 0.0), axis=1, keepdims=True)
    rec = jnp.zeros((tm, LANES), F32)
    for where_, val in ((R_POS1, pos1), (R_POS2, pos2), (R_P1, p1), (R_P2, p2)):
        rec = jnp.where(lane == where_, val, rec)
    route_ref[...] = rec

    slot_lane = lax.broadcasted_iota(jnp.int32, (tm, MOE_ROWS), 1).astype(F32)
    pick_t = jnp.where((slot_lane == pos1) | (slot_lane == pos2), 1.0, 0.0)
    sorted_rows = _dot(pick_t.T.astype(BF16), hf.astype(BF16))

    stage[slot] = _pack_bf16_pairs(sorted_rows)

    carry = carry_scr[0:1, :]
    srow = lax.broadcasted_iota(jnp.int32, (8, LANES), 0)
    elane = lax.broadcasted_iota(jnp.int32, (8, LANES), 1)
    table = jnp.where(srow == S_CNT, cnt, jnp.where(srow == S_LOCAL, local, carry + (elane * cap).astype(F32)))
    tab_v[...] = table.astype(jnp.int32)
    total = carry + cnt
    carry_scr[...] = jnp.broadcast_to(total, carry_scr.shape)
    cnt_ref[0] = jnp.broadcast_to(cnt, (8, LANES))

    tcp = pltpu.make_async_copy(tab_v, tab_s.at[slot], tab_sem)
    tcp.start()
    tcp.wait()

    def runs(s, start):
        _run_copies(tab_s.at[s],
                    lambda e, off, size: stage.at[s, _run_rows(tab_s[s, S_LOCAL, e] + off, size)],
                    lambda e, off, size: xs_ref.at[_run_rows(tab_s[s, S_GLOBAL, e] + off, size)],
                    sem.at[s], start)

    runs(slot, True)

    @pl.when(c > 0)
    def _():
        runs(1 - slot, False)

    @pl.when(c == last)
    def _():
        runs(slot, False)
        zeros_v[...] = jnp.zeros_like(zeros_v)
        pad = (-total.astype(jnp.int32)) & (MOE_TILE - 1)
        tab_v[...] = jnp.where(srow == S_CNT, pad, jnp.where(srow == S_LOCAL, 0,
                                                               total.astype(jnp.int32) + elane * cap))
        pcp = pltpu.make_async_copy(tab_v, tab_s.at[slot], tab_sem)
        pcp.start()
        pcp.wait()
        for start in (True, False):
            _run_copies(tab_s.at[slot],
                        lambda e, off, size: zeros_v.at[_run_rows(off, size)],
                        lambda e, off, size: xs_ref.at[_run_rows(tab_s[slot, S_GLOBAL, e] + off, size)],
                        sem.at[slot], start)


def _expert_region(t):
    worst = t + (t // MOE_CHUNK) * (RUN_ALIGN - 1)
    return -(-worst // MOE_TILE) * MOE_TILE


def _router(x2, fn, router_w):
    t = x2.shape[0]
    tm = MOE_CHUNK
    nc = t // tm
    n_rows = N_EXPERTS * _expert_region(t) + MOE_TILE
    full = lambda a: pl.BlockSpec(a.shape, lambda i: (0,) * a.ndim)
    return pl.pallas_call(
        _router_kernel,
        grid=(nc,),
        in_specs=[pl.BlockSpec((tm, D_MODEL), lambda i: (i, 0)), full(fn), full(router_w)],
        out_specs=[pl.BlockSpec((tm, LANES), lambda i: (i, 0)), pl.BlockSpec((1, 8, LANES), lambda i: (i, 0, 0)),
                   pl.BlockSpec(memory_space=pl.ANY)],
        out_shape=[jax.ShapeDtypeStruct((t, LANES), F32), jax.ShapeDtypeStruct((nc, 8, LANES), F32),
                   jax.ShapeDtypeStruct((n_rows, PACKED), jnp.uint32)],
        scratch_shapes=[pltpu.VMEM((8, LANES), F32), pltpu.VMEM((2, MOE_ROWS, PACKED), jnp.uint32),
                        pltpu.VMEM((8, LANES), jnp.int32), pltpu.SMEM((2, 8, LANES), jnp.int32),
                        pltpu.VMEM((MOE_TILE, PACKED), jnp.uint32),
                        pltpu.SemaphoreType.DMA((2,)), pltpu.SemaphoreType.DMA(())],
        compiler_params=_cparams(1),
        name="moe_router",
    )(x2, fn, router_w)


def _experts_kernel(tb_ref, te_ref, tv_ref, xs_ref, wgu_ref, wd_ref, y_ref, act_scr):
    del tb_ref, te_ref
    j = pl.program_id(0)

    @pl.when(tv_ref[j] != 0)
    def _():
        h = _unpack_bf16_pairs(xs_ref[...]).astype(BF16)
        for c in range(D_FF // FF_CHUNK):
            g = _dot(h, wgu_ref[0, :, c * FF_CHUNK:(c + 1) * FF_CHUNK])
            u = _dot(h, wgu_ref[0, :, D_FF + c * FF_CHUNK:D_FF + (c + 1) * FF_CHUNK])
            act_scr[:, c * FF_CHUNK:(c + 1) * FF_CHUNK] = (g * _sigmoid(g) * u).astype(BF16)
        y_ref[...] = _pack_bf16_pairs(_dot(act_scr[...], wd_ref[0]))

    @pl.when(tv_ref[j] == 0)
    def _():
        y_ref[...] = jnp.zeros_like(y_ref)


def _experts(tile_block, tile_expert, tile_valid, xs, w_gu, w_down):
    tm = MOE_TILE
    row = pl.BlockSpec((tm, PACKED), lambda j, tb, te, tv: (tb[j], 0))
    return pl.pallas_call(
        _experts_kernel,
        grid_spec=pltpu.PrefetchScalarGridSpec(
            num_scalar_prefetch=3,
            grid=(tile_block.shape[0],),
            in_specs=[row,
                      pl.BlockSpec((1, D_MODEL, 2 * D_FF), lambda j, tb, te, tv: (te[j], 0, 0)),
                      pl.BlockSpec((1, D_FF, D_MODEL), lambda j, tb, te, tv: (te[j], 0, 0))],
            out_specs=row,
            scratch_shapes=[pltpu.VMEM((tm, D_FF), BF16)]),
        out_shape=jax.ShapeDtypeStruct(xs.shape, jnp.uint32),
        compiler_params=_cparams(1),
        name="moe_experts",
    )(tile_block, tile_expert, tile_valid, xs, w_gu, w_down)


def _combine_kernel(tcur_ref, tnext_ref, x_ref, route_ref, y_ref, o_ref, ybuf, sem):
    c = pl.program_id(0)
    last = pl.num_programs(0) - 1
    tm = x_ref.shape[0]
    slot = c % 2

    def runs(table, s, start):
        _run_copies(table.at[0],
                    lambda e, off, size: y_ref.at[_run_rows(table[0, S_GLOBAL, e] + off, size)],
                    lambda e, off, size: ybuf.at[s, _run_rows(table[0, S_LOCAL, e] + off, size)],
                    sem.at[s], start)

    @pl.when(c == 0)
    def _():
        ybuf[...] = jnp.zeros_like(ybuf)
        runs(tcur_ref, 0, True)

    @pl.when(c < last)
    def _():
        runs(tnext_ref, 1 - slot, True)

    runs(tcur_ref, slot, False)

    rec = route_ref[...]
    ys = _unpack_bf16_pairs(ybuf[slot]).astype(BF16)
    slot_lane = lax.broadcasted_iota(jnp.int32, (tm, MOE_ROWS), 1).astype(F32)
    out = x_ref[...]
    for pos_lane, p_lane in ((R_POS1, R_P1), (R_POS2, R_P2)):
        pick = jnp.where(slot_lane == rec[:, pos_lane:pos_lane + 1], 1.0, 0.0).astype(BF16)
        out = out + rec[:, p_lane:p_lane + 1] * _dot(pick, ys)
    o_ref[...] = out


def _combine(tables, x2, route, y):
    t = x2.shape[0]
    tm = MOE_CHUNK
    nc = t // tm
    tok = lambda w: pl.BlockSpec((tm, w), lambda c: (c, 0))
    tab = lambda f: pl.BlockSpec((1, 3, N_EXPERTS), lambda c: (f(c), 0, 0), memory_space=pltpu.SMEM)
    return pl.pallas_call(
        _combine_kernel,
        grid=(nc,),
        in_specs=[tab(lambda c: c), tab(lambda c: jnp.minimum(c + 1, nc - 1)),
                  tok(D_MODEL), tok(LANES), pl.BlockSpec(memory_space=pl.ANY)],
        out_specs=tok(D_MODEL),
        out_shape=jax.ShapeDtypeStruct((t, D_MODEL), F32),
        scratch_shapes=[pltpu.VMEM((2, MOE_ROWS, PACKED), jnp.uint32), pltpu.SemaphoreType.DMA((2,))],
        compiler_params=_cparams(1),
        name="moe_combine",
    )(tables, tables, x2, route, y)


def _ffn_moe(x2, fn, router_w, w_gu, w_down):
    t = x2.shape[0]
    route, cnt, xs = _router(x2, fn, router_w)
    counts = cnt[:, 0, :N_EXPERTS].astype(jnp.int32)
    upto = jnp.cumsum(counts, axis=0)
    experts = jnp.arange(N_EXPERTS, dtype=jnp.int32)
    region = _expert_region(t)
    tables = jnp.stack([counts, jnp.cumsum(counts, axis=1) - counts,
                        upto - counts + experts[None, :] * region], axis=1)
    tiles = (upto[-1] + MOE_TILE - 1) // MOE_TILE
    tile_end = jnp.cumsum(tiles)
    max_rows = 2 * t + (t // MOE_CHUNK) * N_EXPERTS * (RUN_ALIGN - 1)
    n_steps = max_rows // MOE_TILE + N_EXPERTS
    j = jnp.arange(n_steps, dtype=jnp.int32)
    tile_expert = jnp.minimum(jnp.sum((j[:, None] >= tile_end[None, :]).astype(jnp.int32), axis=1), N_EXPERTS - 1)
    first_tile = jnp.sum(jnp.where(experts[None, :] == tile_expert[:, None], (tile_end - tiles)[None, :], 0), axis=1)
    tile_valid = (j < tile_end[-1]).astype(jnp.int32)
    spare = N_EXPERTS * region // MOE_TILE
    tile_block = jnp.where(tile_valid != 0, tile_expert * (region // MOE_TILE) + (j - first_tile), spare)
    y = _experts(tile_block, tile_expert, tile_valid, xs, w_gu, w_down)
    return _combine(tables, x2, route, y)


def _slot_cols(w, dims):
    k = w.shape[0]
    w = w.reshape(k, N_HEADS, dims)
    return jnp.pad(w, ((0, 0), (0, 0), (0, SLOT - dims))).reshape(k, HW)


def _slot_vec(g, dims):
    return jnp.tile(jnp.pad(g, (0, SLOT - dims)), N_HEADS).reshape(1, HW)


def _rope_table(seq, rot_dim, lane0):
    half = rot_dim // 2
    inv = ROPE_THETA ** (-jnp.arange(0, rot_dim, 2, dtype=F32) / rot_dim)
    ang = jnp.arange(seq, dtype=F32)[:, None] * inv[None, :]
    cos, sin = jnp.cos(ang), jnp.sin(ang)
    c = jnp.ones((seq, SLOT), F32).at[:, lane0:lane0 + half].set(cos).at[:, lane0 + half:lane0 + rot_dim].set(cos)
    s_fwd = jnp.zeros((seq, SLOT), F32).at[:, lane0 + half:lane0 + rot_dim].set(sin)
    s_bwd = jnp.zeros((seq, SLOT), F32).at[:, lane0:lane0 + half].set(-sin)
    return jnp.stack([c, s_fwd, s_bwd])


def _constants():
    r = np.arange(MXU_DIM)
    seg = (r[:, None] // SLOT == r[None, :] // SLOT).astype(np.float32)
    place = np.zeros((N_HEADS * MAX_MOBA_BLOCKS, HW), np.float32)
    for h in range(N_HEADS):
        for n in range(MAX_MOBA_BLOCKS):
            place[h * MAX_MOBA_BLOCKS + n, h * SLOT + BIAS_LANE0 + n] = 1.0
    return jnp.asarray(seg, BF16), jnp.asarray(place, BF16)


def _layer_params(l, attn_norm, w_in, moba_q_norm, moba_k_norm, mla_cq_norm, w_uq, mla_ckv_norm,
                  w_ukv, mla_q_norm, mla_k_norm, w_branch_a, w_branch_b, w_out):
    wi = w_in[l]
    aw = N_HEADS * A_HEAD_DIM
    o_cq = 3 * aw
    o_ckv = o_cq + B_Q_RANK
    o_kr = o_ckv + B_KV_RANK
    o_g = o_kr + B_ROPE_DIM
    w_proj = jnp.concatenate([
        _slot_cols(wi[:, 0:aw], A_HEAD_DIM), _slot_cols(wi[:, aw:2 * aw], A_HEAD_DIM),
        _slot_cols(wi[:, 2 * aw:3 * aw], A_HEAD_DIM),
        wi[:, o_cq:o_ckv], wi[:, o_ckv:o_kr], jnp.pad(wi[:, o_kr:o_g], ((0, 0), (0, LANES - B_ROPE_DIM)))], axis=1)
    ukv = w_ukv[l].reshape(B_KV_RANK, N_HEADS, B_NOPE_DIM + B_V_DIM)
    w_ukv_s = jnp.concatenate([_slot_cols(ukv[:, :, :B_NOPE_DIM].reshape(B_KV_RANK, -1), B_NOPE_DIM),
                               _slot_cols(ukv[:, :, B_NOPE_DIM:].reshape(B_KV_RANK, -1), B_V_DIM)], axis=1)
    seg, place = _constants()
    return {
        "attn_norm": attn_norm[l].reshape(1, D_MODEL),
        "w_proj": w_proj.astype(BF16),
        "w_gates": wi[:, o_g:].astype(BF16),
        "w_uq": _slot_cols(w_uq[l], B_QK_DIM).astype(BF16),
        "w_ukv": w_ukv_s.astype(BF16),
        "cq_norm": mla_cq_norm[l].reshape(1, B_Q_RANK),
        "ckv_norm": mla_ckv_norm[l].reshape(1, B_KV_RANK),
        "qn_a": _slot_vec(moba_q_norm[l], A_HEAD_DIM),
        "kn_a": _slot_vec(moba_k_norm[l], A_HEAD_DIM),
        "qn_b": _slot_vec(mla_q_norm[l], B_QK_DIM),
        "kn_b": _slot_vec(mla_k_norm[l], B_QK_DIM),
        "seg": seg,
        "place": place,
        "w_br_a": w_branch_a[l].astype(BF16),
        "w_br_b": w_branch_b[l].astype(BF16),
        "w_out": w_out[l].astype(BF16),
    }


def kernel(x, attn_norm, w_in, moba_q_norm, moba_k_norm, mla_cq_norm, w_uq, mla_ckv_norm, w_ukv,
           mla_q_norm, mla_k_norm, w_branch_a, w_branch_b, w_out, ffn_norm, dense_w_gate_up,
           dense_w_down, router_w, expert_w_gate_up, expert_w_down):
    b, s, d = x.shape
    depth = attn_norm.shape[0]
    rope_a = _rope_table(s, A_ROT_DIM, 0)
    rope_b = _rope_table(s, B_ROPE_DIM, B_NOPE_DIM)
    for l in range(depth):
        p = _layer_params(l, attn_norm, w_in, moba_q_norm, moba_k_norm, mla_cq_norm, w_uq,
                          mla_ckv_norm, w_ukv, mla_q_norm, mla_k_norm, w_branch_a, w_branch_b, w_out)
        qa, ka, va, qb, kb, vb = _prologue(x, p, rope_a, rope_b)
        ya = _attention(qa, ka, va)
        yb = _attention(qb, kb, vb)
        x2 = _merge(x.reshape(b * s, d), p, ya.reshape(b * s, V_WIDTH), yb.reshape(b * s, V_WIDTH))
        fn = ffn_norm[l].reshape(1, D_MODEL)
        if l % 2 == 0:
            x2 = _ffn_dense(x2, fn, dense_w_gate_up[l // 2].astype(BF16), dense_w_down[l // 2].astype(BF16))
        else:
            rw = jnp.pad(router_w[l // 2], ((0, 0), (0, LANES - N_EXPERTS)))
            x2 = _ffn_moe(x2, fn, rw, expert_w_gate_up[l // 2].astype(BF16), expert_w_down[l // 2].astype(BF16))
        x = x2.reshape(b, s, d)
    return x
```

```python
import math

import jax
import jax.numpy as jnp
import numpy as np
from jax import lax
from jax.experimental import pallas as pl
from jax.experimental.pallas import tpu as pltpu

F32 = jnp.float32
BF16 = jnp.bfloat16

D_MODEL = 1024
N_HEADS = 8
A_HEAD_DIM = 64
A_ROT_DIM = 16
MOBA_BLOCK = 256
MOBA_TOPK = 3
MAX_MOBA_BLOCKS = 16
B_NOPE_DIM = 64
B_ROPE_DIM = 32
B_QK_DIM = B_NOPE_DIM + B_ROPE_DIM
B_V_DIM = 64
B_Q_RANK = 384
B_KV_RANK = 256
ROPE_THETA = 500000.0
EPS = 1e-6
NEG = -1e30
D_FF = 2816
N_EXPERTS = 8

LANES = 128
SLOT = LANES
MXU_DIM = 256
HW = N_HEADS * SLOT
BIAS_LANE0 = A_HEAD_DIM
V_WIDTH = N_HEADS * B_V_DIM

C_QA, C_KA, C_VA = 0, HW, 2 * HW
C_CQ = C_VA + HW
C_CKV = C_CQ + B_Q_RANK
C_KR = C_CKV + B_KV_RANK
N_PROJ = C_KR + LANES
V_LANE0 = (0, B_V_DIM)
SUM_LANE = (B_V_DIM, 0)
LOG2E = math.log2(math.e)

VMEM_LIMIT = 56 * 1024 * 1024


def _cparams(n_axes):
    return pltpu.CompilerParams(dimension_semantics=("arbitrary",) * n_axes,
                                vmem_limit_bytes=VMEM_LIMIT)


def _dot(a, b):
    return jnp.dot(a, b, preferred_element_type=F32)


def _dot_nt(a, b):
    return lax.dot_general(a, b, (((1,), (1,)), ((), ())), preferred_element_type=F32)


def _split_bf16(a):
    hi = a.astype(BF16)
    lo = (a - hi.astype(F32)).astype(BF16)
    return hi, lo


def _dot3(a, b):
    ah, al = _split_bf16(a)
    bh, bl = _split_bf16(b)
    return _dot(ah, bh) + (_dot(ah, bl) + _dot(al, bh))


def _dot3_nt(a, b):
    ah, al = _split_bf16(a)
    bh, bl = _split_bf16(b)
    return _dot_nt(ah, bh) + (_dot_nt(ah, bl) + _dot_nt(al, bh))


def _rms_rows(xf, g):
    ms = jnp.mean(xf * xf, axis=-1, keepdims=True)
    return xf * lax.rsqrt(ms + EPS) * g


def _sigmoid(z):
    return 1.0 / (1.0 + jnp.exp(-z))


PROLOGUE_BLOCKS = 2


def _slot_norm_rope(x, g_ref, seg_ref, inv_n, rope, shift):
    cos, sin_fwd, sin_bwd = rope
    slots = []
    for c in range(HW // MXU_DIM):
        xc = x[:, c * MXU_DIM:(c + 1) * MXU_DIM]
        ss = _dot((xc * xc).astype(BF16), seg_ref[...])
        xn = xc * lax.rsqrt(ss * inv_n + EPS) * g_ref[:, c * MXU_DIM:(c + 1) * MXU_DIM]
        for j in range(MXU_DIM // SLOT):
            xs = xn[:, j * SLOT:(j + 1) * SLOT]
            slots.append(xs * cos + pltpu.roll(xs, shift, 1) * sin_fwd
                         + pltpu.roll(xs, SLOT - shift, 1) * sin_bwd)
    return slots


def _prologue_kernel(x_ref, an_ref, w_ref, wuq_ref, wukv_ref, cqn_ref, ckvn_ref,
                     qna_ref, kna_ref, qnb_ref, knb_ref, seg_ref, place_ref,
                     ropea_ref, ropeb_ref,
                     qa_ref, ka_ref, va_ref, qb_ref, kb_ref, vb_ref, kmean_scr):
    @pl.when(pl.program_id(1) == 0)
    def _():
        kmean_scr[...] = jnp.zeros_like(kmean_scr)

    out_refs = (qa_ref, ka_ref, va_ref, qb_ref, kb_ref, vb_ref)
    for j in range(x_ref.shape[1] // MOBA_BLOCK):
        rows = slice(j * MOBA_BLOCK, (j + 1) * MOBA_BLOCK)
        hn = _rms_rows(x_ref[0, rows], an_ref[...]).astype(BF16)
        proj = _dot(hn, w_ref[...])
        _prologue_block(proj, pl.program_id(1) * (x_ref.shape[1] // MOBA_BLOCK) + j, rows,
                        wuq_ref, wukv_ref, cqn_ref, ckvn_ref, qna_ref, kna_ref, qnb_ref, knb_ref,
                        seg_ref, place_ref, [ropea_ref[t, rows] for t in range(3)],
                        [ropeb_ref[t, rows] for t in range(3)], out_refs, kmean_scr)


def _prologue_block(proj, s, rows, wuq_ref, wukv_ref, cqn_ref, ckvn_ref, qna_ref, kna_ref, qnb_ref, knb_ref,
                    seg_ref, place_ref, rope_a, rope_b, out_refs, kmean_scr):
    qa_ref, ka_ref, va_ref, qb_ref, kb_ref, vb_ref = out_refs
    tm = MOBA_BLOCK

    qa = _slot_norm_rope(proj[:, C_QA:C_QA + HW], qna_ref, seg_ref, 1.0 / A_HEAD_DIM, rope_a, A_ROT_DIM // 2)
    ka = _slot_norm_rope(proj[:, C_KA:C_KA + HW], kna_ref, seg_ref, 1.0 / A_HEAD_DIM, rope_a, A_ROT_DIM // 2)

    kmean = kmean_scr[...]
    gates = [_dot3_nt(kmean[:, h * SLOT:(h + 1) * SLOT], qa[h]) for h in range(N_HEADS)]
    g3 = jnp.stack(gates, axis=0)
    blk = lax.broadcasted_iota(jnp.int32, g3.shape, 1)
    g3 = jnp.where(blk < s, g3, -jnp.inf)
    rank3 = jnp.zeros(g3.shape, F32)
    for n in range(MAX_MOBA_BLOCKS):
        row = g3[:, n:n + 1, :]
        rank3 = rank3 + jnp.where((row > g3) | ((row == g3) & (blk > n)), 1.0, 0.0)
    sel3 = jnp.where(((rank3 < MOBA_TOPK) & (blk < s)) | (blk == s), 1.0, 0.0)
    sel_t = sel3.reshape(N_HEADS * MAX_MOBA_BLOCKS, tm)
    placed = _dot(sel_t.T.astype(BF16), place_ref[...])
    lane = lax.broadcasted_iota(jnp.int32, (tm, SLOT), 1)
    is_bias = (lane >= BIAS_LANE0) & (lane < BIAS_LANE0 + MAX_MOBA_BLOCKS)
    scale_a = LOG2E / math.sqrt(A_HEAD_DIM)
    for h in range(N_HEADS):
        sl = slice(h * SLOT, (h + 1) * SLOT)
        bias = (placed[:, sl] - 1.0) * (-NEG)
        qa_ref[0, rows, sl] = jnp.where(is_bias, bias, qa[h] * scale_a).astype(BF16)
        ka_ref[0, rows, sl] = jnp.where(lane == BIAS_LANE0 + s, 1.0, ka[h]).astype(BF16)
        va_ref[0, rows, sl] = jnp.where(lane == SUM_LANE[h % 2], 1.0,
                                        proj[:, C_VA + h * SLOT:C_VA + (h + 1) * SLOT]).astype(BF16)

    km = jnp.concatenate([jnp.mean(k, axis=0, keepdims=True) for k in ka], axis=1)
    kmean_scr[pl.ds(s, 1), :] = km

    cq = _rms_rows(proj[:, C_CQ:C_CQ + B_Q_RANK], cqn_ref[...]).astype(BF16)
    qb = _dot(cq, wuq_ref[...])
    ckv = _rms_rows(proj[:, C_CKV:C_CKV + B_KV_RANK], ckvn_ref[...]).astype(BF16)
    kv = _dot(ckv, wukv_ref[...])
    kr = pltpu.roll(proj[:, C_KR:C_KR + LANES], B_NOPE_DIM, 1)
    kb = kv[:, :HW] + jnp.concatenate([kr] * N_HEADS, axis=1)
    scale_b = LOG2E / math.sqrt(B_QK_DIM)
    qb_s = _slot_norm_rope(qb, qnb_ref, seg_ref, 1.0 / B_QK_DIM, rope_b, B_ROPE_DIM // 2)
    kb_s = _slot_norm_rope(kb, knb_ref, seg_ref, 1.0 / B_QK_DIM, rope_b, B_ROPE_DIM // 2)
    for h in range(N_HEADS):
        sl = slice(h * SLOT, (h + 1) * SLOT)
        qb_ref[0, rows, sl] = (qb_s[h] * scale_b).astype(BF16)
        kb_ref[0, rows, sl] = kb_s[h].astype(BF16)
        vb_ref[0, rows, sl] = jnp.where(lane == SUM_LANE[h % 2], 1.0,
                                        kv[:, HW + h * SLOT:HW + (h + 1) * SLOT]).astype(BF16)


def _prologue(x, p, rope_a, rope_b):
    b, s, _ = x.shape
    tm = PROLOGUE_BLOCKS * MOBA_BLOCK
    assert s % tm == 0 and s // MOBA_BLOCK <= MAX_MOBA_BLOCKS
    full = lambda a: pl.BlockSpec(a.shape, lambda i, j: (0,) * a.ndim)
    tok = lambda w: pl.BlockSpec((1, tm, w), lambda i, j: (i, j, 0))
    rope = pl.BlockSpec((3, tm, LANES), lambda i, j: (0, j, 0))
    consts = [p["attn_norm"], p["w_proj"], p["w_uq"], p["w_ukv"], p["cq_norm"], p["ckv_norm"],
              p["qn_a"], p["kn_a"], p["qn_b"], p["kn_b"], p["seg"], p["place"]]
    out_shape = [jax.ShapeDtypeStruct((b, s, HW), BF16)] * 6
    return pl.pallas_call(
        _prologue_kernel,
        grid=(b, s // tm),
        in_specs=[tok(D_MODEL)] + [full(a) for a in consts] + [rope, rope],
        out_specs=[tok(HW)] * 6,
        out_shape=out_shape,
        scratch_shapes=[pltpu.VMEM((MAX_MOBA_BLOCKS, HW), F32)],
        compiler_params=_cparams(2),
        name="prologue",
    )(x, *consts, rope_a, rope_b)


ATTN_TILE = 1024
ATTN_HEADS = 4
DIAG_SPLIT = 2


def _attn_kernel(q_ref, k_ref, v_ref, o_ref, m_scr, acc_scr):
    i = pl.program_id(2)
    t = q_ref.shape[1]
    tr = t // DIAG_SPLIT

    def scores(n, hh):
        sl = slice(hh * SLOT, (hh + 1) * SLOT)
        off = pl.multiple_of(n * t, t)
        return _dot_nt(q_ref[0, :, sl], k_ref[0, pl.ds(off, t), sl]), v_ref[0, pl.ds(off, t), sl]

    diag = pl.multiple_of(i * t, t)
    for hh in range(ATTN_HEADS):
        sl = slice(hh * SLOT, (hh + 1) * SLOT)
        for r in range(DIAG_SPLIT):
            rows = slice(r * tr, (r + 1) * tr)
            nk = (r + 1) * tr
            sc = _dot_nt(q_ref[0, rows, sl], k_ref[0, pl.ds(diag, nk), sl])
            row = lax.broadcasted_iota(jnp.int32, (tr, nk), 0)
            col = lax.broadcasted_iota(jnp.int32, (tr, nk), 1)
            sc = jnp.where(col <= row + r * tr, sc, NEG)
            m = jnp.max(sc, axis=1, keepdims=True)
            m_scr[hh, rows] = m
            acc_scr[hh, rows] = _dot(jnp.exp2(sc - m).astype(BF16), v_ref[0, pl.ds(diag, nk), sl])
    first = [(jnp.max(jnp.broadcast_to(m_scr[hh], (t, LANES)), axis=1, keepdims=True), acc_scr[hh])
             for hh in range(ATTN_HEADS)]

    def past_tile(n, state):
        new = []
        for hh, (m, acc) in enumerate(state):
            sc, v = scores(n, hh)
            m_new = jnp.maximum(m, jnp.max(sc, axis=1, keepdims=True))
            new.append((m_new, jnp.exp2(m - m_new) * acc + _dot(jnp.exp2(sc - m_new).astype(BF16), v)))
        return tuple(new)

    last = lax.fori_loop(0, i, past_tile, tuple(first))
    outs = [acc / acc[:, SUM_LANE[hh % 2]:SUM_LANE[hh % 2] + 1] for hh, (_, acc) in enumerate(last)]
    lane = lax.broadcasted_iota(jnp.int32, (t, LANES), 1)
    for p in range(ATTN_HEADS // 2):
        pair = jnp.where(lane < B_V_DIM, outs[2 * p], outs[2 * p + 1])
        o_ref[0, :, p * LANES:(p + 1) * LANES] = pair.astype(o_ref.dtype)


def _attention(q, k, v):
    b, s, _ = q.shape
    t = min(ATTN_TILE, s)
    assert s % t == 0 and t % MOBA_BLOCK == 0
    return pl.pallas_call(
        _attn_kernel,
        grid=(b, N_HEADS // ATTN_HEADS, s // t),
        in_specs=[pl.BlockSpec((1, t, ATTN_HEADS * SLOT), lambda bi, p, i: (bi, i, p)),
                  pl.BlockSpec((1, s, ATTN_HEADS * SLOT), lambda bi, p, i: (bi, 0, p)),
                  pl.BlockSpec((1, s, ATTN_HEADS * SLOT), lambda bi, p, i: (bi, 0, p))],
        out_specs=pl.BlockSpec((1, t, ATTN_HEADS * B_V_DIM), lambda bi, p, i: (bi, i, p)),
        out_shape=jax.ShapeDtypeStruct((b, s, V_WIDTH), BF16),
        scratch_shapes=[pltpu.VMEM((ATTN_HEADS, t, 1), F32), pltpu.VMEM((ATTN_HEADS, t, LANES), F32)],
        compiler_params=_cparams(3),
        name="attention",
    )(q, k, v)


def _merge_kernel(x_ref, an_ref, ya_ref, yb_ref, wg_ref, wa_ref, wb_ref, wo_ref, o_ref):
    xf = x_ref[...]
    hn = _rms_rows(xf, an_ref[...]).astype(BF16)
    gates = _dot(hn, wg_ref[...])
    pa = _dot(ya_ref[...], wa_ref[...])
    pb = _dot(yb_ref[...], wb_ref[...])
    merged = _sigmoid(gates[:, :D_MODEL]) * pa + _sigmoid(gates[:, D_MODEL:]) * pb
    o_ref[...] = xf + _dot(merged.astype(BF16), wo_ref[...])


def _merge(x2, p, ya2, yb2, tm=512):
    t = x2.shape[0]
    full = lambda a: pl.BlockSpec(a.shape, lambda i: (0,) * a.ndim)
    tok = lambda w: pl.BlockSpec((tm, w), lambda i: (i, 0))
    consts = [p["w_gates"], p["w_br_a"], p["w_br_b"], p["w_out"]]
    return pl.pallas_call(
        _merge_kernel,
        grid=(t // tm,),
        in_specs=[tok(D_MODEL), full(p["attn_norm"]), tok(V_WIDTH), tok(V_WIDTH)] + [full(a) for a in consts],
        out_specs=tok(D_MODEL),
        out_shape=jax.ShapeDtypeStruct((t, D_MODEL), F32),
        compiler_params=_cparams(1),
        name="merge",
    )(x2, p["attn_norm"], ya2, yb2, *consts)


FF_CHUNK = MXU_DIM


def _ffn_kernel(x_ref, fn_ref, wgu_ref, wd_ref, o_ref, act_scr):
    xf = x_ref[...]
    h = _rms_rows(xf, fn_ref[...]).astype(BF16)
    for c in range(D_FF // FF_CHUNK):
        g = _dot(h, wgu_ref[:, c * FF_CHUNK:(c + 1) * FF_CHUNK])
        u = _dot(h, wgu_ref[:, D_FF + c * FF_CHUNK:D_FF + (c + 1) * FF_CHUNK])
        act_scr[:, c * FF_CHUNK:(c + 1) * FF_CHUNK] = (g * _sigmoid(g) * u).astype(BF16)
    o_ref[...] = xf + _dot(act_scr[...], wd_ref[...])


def _ffn_dense(x2, fn, w_gu, w_down, tm=512):
    t = x2.shape[0]
    full = lambda a: pl.BlockSpec(a.shape, lambda i: (0,) * a.ndim)
    tok = pl.BlockSpec((tm, D_MODEL), lambda i: (i, 0))
    return pl.pallas_call(
        _ffn_kernel,
        grid=(t // tm,),
        in_specs=[tok, full(fn), full(w_gu), full(w_down)],
        out_specs=tok,
        out_shape=jax.ShapeDtypeStruct((t, D_MODEL), F32),
        scratch_shapes=[pltpu.VMEM((tm, D_FF), BF16)],
        compiler_params=_cparams(1),
        name="ffn_dense",
    )(x2, fn, w_gu, w_down)


MOE_TILE = 512
MOE_CHUNK = 512
RUN_ALIGN = 8
MOE_ROWS = -(-(2 * MOE_CHUNK + N_EXPERTS * RUN_ALIGN) // LANES) * LANES
RUN_BITS = range(RUN_ALIGN.bit_length() - 1, (2 * MOE_CHUNK).bit_length())
PACKED = D_MODEL // 2
R_POS1, R_POS2, R_P1, R_P2 = range(4)
S_CNT, S_LOCAL, S_GLOBAL = range(3)


def _pack_bf16_pairs(xf):
    bits = lax.bitcast_convert_type(xf.astype(BF16).astype(F32), jnp.uint32)
    half = xf.shape[1] // 2
    return bits[:, :half] | (bits[:, half:] >> 16)


def _unpack_bf16_pairs(w):
    hi = lax.bitcast_convert_type(w & jnp.uint32(0xFFFF0000), F32)
    lo = lax.bitcast_convert_type(w << 16, F32)
    return jnp.concatenate([hi, lo], axis=1)


def _run_rows(first, size):
    return pl.ds(pl.multiple_of(first, RUN_ALIGN), size)


def _run_copies(table, src_of, dst_of, sem, start):
    for e in range(N_EXPERTS):
        n = table[S_CNT, e]
        for bit in RUN_BITS:
            size = 1 << bit
            above = n & ~(2 * size - 1)

            @pl.when((n & size) != 0)
            def _(e=e, size=size, above=above):
                cp = pltpu.make_async_copy(src_of(e, above, size), dst_of(e, above, size), sem)
                if start:
                    cp.start(priority=e % 2)
                else:
                    cp.wait()


def _router_kernel(x_ref, fn_ref, rw_ref, route_ref, cnt_ref, xs_ref,
                   carry_scr, stage, tab_v, tab_s, zeros_v, sem, tab_sem):
    c = pl.program_id(0)
    last = pl.num_programs(0) - 1
    tm = x_ref.shape[0]
    slot = c % 2
    cap = (xs_ref.shape[0] - MOE_TILE) // N_EXPERTS

    @pl.when(c == 0)
    def _():
        carry_scr[...] = jnp.zeros_like(carry_scr)

    hf = _rms_rows(x_ref[...], fn_ref[...])
    lane = lax.broadcasted_iota(jnp.int32, (tm, LANES), 1)
    logits = jnp.where(lane < N_EXPERTS, _dot3(hf, rw_ref[...]), -jnp.inf)
    m1 = jnp.max(logits, axis=1, keepdims=True)
    i1 = jnp.min(jnp.where(logits == m1, lane, LANES), axis=1, keepdims=True)
    rest = jnp.where(lane == i1, -jnp.inf, logits)
    m2 = jnp.max(rest, axis=1, keepdims=True)
    i2 = jnp.min(jnp.where(rest == m2, lane, LANES), axis=1, keepdims=True)
    e2 = jnp.exp(m2 - m1)
    p1 = 1.0 / (1.0 + e2)
    p2 = e2 / (1.0 + e2)

    onehot = jnp.where((lane == i1) | (lane == i2), 1.0, 0.0)
    r = lax.broadcasted_iota(jnp.int32, (tm, tm), 0)
    cc = lax.broadcasted_iota(jnp.int32, (tm, tm), 1)
    earlier = jnp.where(cc < r, 1.0, 0.0).astype(BF16)
    before = _dot(earlier, onehot.astype(BF16))
    er = lax.broadcasted_iota(jnp.int32, (LANES, LANES), 0)
    ec = lax.broadcasted_iota(jnp.int32, (LANES, LANES), 1)
    cnt = jnp.floor((jnp.sum(onehot, axis=0, keepdims=True) + (RUN_ALIGN - 1)) * (1.0 / RUN_ALIGN)) * RUN_ALIGN
    local = _dot(jnp.broadcast_to(cnt, (8, LANES)).astype(BF16), jnp.where(er < ec, 1.0, 0.0).astype(BF16))[0:1]
    pos = before + local
    pos1 = jnp.sum(jnp.where(lane == i1, pos, 0.0), axis=1, keepdims=True)
    pos2 = jnp.sum(jnp.where(lane == i2, pos, 0.0), axis=1, keepdims=True)
    rec = jnp.zeros((tm, LANES), F32)
    for where_, val in ((R_POS1, pos1), (R_POS2, pos2), (R_P1, p1), (R_P2, p2)):
        rec = jnp.where(lane == where_, val, rec)
    route_ref[...] = rec

    slot_lane = lax.broadcasted_iota(jnp.int32, (tm, MOE_ROWS), 1).astype(F32)
    pick_t = jnp.where((slot_lane == pos1) | (slot_lane == pos2), 1.0, 0.0)
    sorted_rows = _dot(pick_t.T.astype(BF16), hf.astype(BF16))

    stage[slot] = _pack_bf16_pairs(sorted_rows)

    carry = carry_scr[0:1, :]
    srow = lax.broadcasted_iota(jnp.int32, (8, LANES), 0)
    elane = lax.broadcasted_iota(jnp.int32, (8, LANES), 1)
    table = jnp.where(srow == S_CNT, cnt, jnp.where(srow == S_LOCAL, local, carry + (elane * cap).astype(F32)))
    tab_v[...] = table.astype(jnp.int32)
    total = carry + cnt
    carry_scr[...] = jnp.broadcast_to(total, carry_scr.shape)
    cnt_ref[0] = jnp.broadcast_to(cnt, (8, LANES))

    tcp = pltpu.make_async_copy(tab_v, tab_s.at[slot], tab_sem)
    tcp.start()
    tcp.wait()

    def runs(s, start):
        _run_copies(tab_s.at[s],
                    lambda e, off, size: stage.at[s, _run_rows(tab_s[s, S_LOCAL, e] + off, size)],
                    lambda e, off, size: xs_ref.at[_run_rows(tab_s[s, S_GLOBAL, e] + off, size)],
                    sem.at[s], start)

    runs(slot, True)

    @pl.when(c > 0)
    def _():
        runs(1 - slot, False)

    @pl.when(c == last)
    def _():
        runs(slot, False)
        zeros_v[...] = jnp.zeros_like(zeros_v)
        pad = (-total.astype(jnp.int32)) & (MOE_TILE - 1)
        tab_v[...] = jnp.where(srow == S_CNT, pad, jnp.where(srow == S_LOCAL, 0,
                                                               total.astype(jnp.int32) + elane * cap))
        pcp = pltpu.make_async_copy(tab_v, tab_s.at[slot], tab_sem)
        pcp.start()
        pcp.wait()
        for start in (True, False):
            _run_copies(tab_s.at[slot],
                        lambda e, off, size: zeros_v.at[_run_rows(off, size)],
                        lambda e, off, size: xs_ref.at[_run_rows(tab_s[slot, S_GLOBAL, e] + off, size)],
                        sem.at[slot], start)


def _expert_region(t):
    worst = t + (t // MOE_CHUNK) * (RUN_ALIGN - 1)
    return -(-worst // MOE_TILE) * MOE_TILE


def _router(x2, fn, router_w):
    t = x2.shape[0]
    tm = MOE_CHUNK
    nc = t // tm
    n_rows = N_EXPERTS * _expert_region(t) + MOE_TILE
    full = lambda a: pl.BlockSpec(a.shape, lambda i: (0,) * a.ndim)
    return pl.pallas_call(
        _router_kernel,
        grid=(nc,),
        in_specs=[pl.BlockSpec((tm, D_MODEL), lambda i: (i, 0)), full(fn), full(router_w)],
        out_specs=[pl.BlockSpec((tm, LANES), lambda i: (i, 0)), pl.BlockSpec((1, 8, LANES), lambda i: (i, 0, 0)),
                   pl.BlockSpec(memory_space=pl.ANY)],
        out_shape=[jax.ShapeDtypeStruct((t, LANES), F32), jax.ShapeDtypeStruct((nc, 8, LANES), F32),
                   jax.ShapeDtypeStruct((n_rows, PACKED), jnp.uint32)],
        scratch_shapes=[pltpu.VMEM((8, LANES), F32), pltpu.VMEM((2, MOE_ROWS, PACKED), jnp.uint32),
                        pltpu.VMEM((8, LANES), jnp.int32), pltpu.SMEM((2, 8, LANES), jnp.int32),
                        pltpu.VMEM((MOE_TILE, PACKED), jnp.uint32),
                        pltpu.SemaphoreType.DMA((2,)), pltpu.SemaphoreType.DMA(())],
        compiler_params=_cparams(1),
        name="moe_router",
    )(x2, fn, router_w)


def _experts_kernel(tb_ref, te_ref, tv_ref, xs_ref, wgu_ref, wd_ref, y_ref, act_scr):
    del tb_ref, te_ref
    j = pl.program_id(0)

    @pl.when(tv_ref[j] != 0)
    def _():
        h = _unpack_bf16_pairs(xs_ref[...]).astype(BF16)
        for c in range(D_FF // FF_CHUNK):
            g = _dot(h, wgu_ref[0, :, c * FF_CHUNK:(c + 1) * FF_CHUNK])
            u = _dot(h, wgu_ref[0, :, D_FF + c * FF_CHUNK:D_FF + (c + 1) * FF_CHUNK])
            act_scr[:, c * FF_CHUNK:(c + 1) * FF_CHUNK] = (g * _sigmoid(g) * u).astype(BF16)
        y_ref[...] = _pack_bf16_pairs(_dot(act_scr[...], wd_ref[0]))

    @pl.when(tv_ref[j] == 0)
    def _():
        y_ref[...] = jnp.zeros_like(y_ref)


def _experts(tile_block, tile_expert, tile_valid, xs, w_gu, w_down):
    tm = MOE_TILE
    row = pl.BlockSpec((tm, PACKED), lambda j, tb, te, tv: (tb[j], 0))
    return pl.pallas_call(
        _experts_kernel,
        grid_spec=pltpu.PrefetchScalarGridSpec(
            num_scalar_prefetch=3,
            grid=(tile_block.shape[0],),
            in_specs=[row,
                      pl.BlockSpec((1, D_MODEL, 2 * D_FF), lambda j, tb, te, tv: (te[j], 0, 0)),
                      pl.BlockSpec((1, D_FF, D_MODEL), lambda j, tb, te, tv: (te[j], 0, 0))],
            out_specs=row,
            scratch_shapes=[pltpu.VMEM((tm, D_FF), BF16)]),
        out_shape=jax.ShapeDtypeStruct(xs.shape, jnp.uint32),
        compiler_params=_cparams(1),
        name="moe_experts",
    )(tile_block, tile_expert, tile_valid, xs, w_gu, w_down)


def _combine_kernel(tcur_ref, tnext_ref, x_ref, route_ref, y_ref, o_ref, ybuf, sem):
    c = pl.program_id(0)
    last = pl.num_programs(0) - 1
    tm = x_ref.shape[0]
    slot = c % 2

    def runs(table, s, start):
        _run_copies(table.at[0],
                    lambda e, off, size: y_ref.at[_run_rows(table[0, S_GLOBAL, e] + off, size)],
                    lambda e, off, size: ybuf.at[s, _run_rows(table[0, S_LOCAL, e] + off, size)],
                    sem.at[s], start)

    @pl.when(c == 0)
    def _():
        ybuf[...] = jnp.zeros_like(ybuf)
        runs(tcur_ref, 0, True)

    @pl.when(c < last)
    def _():
        runs(tnext_ref, 1 - slot, True)

    runs(tcur_ref, slot, False)

    rec = route_ref[...]
    ys = _unpack_bf16_pairs(ybuf[slot]).astype(BF16)
    slot_lane = lax.broadcasted_iota(jnp.int32, (tm, MOE_ROWS), 1).astype(F32)
    out = x_ref[...]
    for pos_lane, p_lane in ((R_POS1, R_P1), (R_POS2, R_P2)):
        pick = jnp.where(slot_lane == rec[:, pos_lane:pos_lane + 1], 1.0, 0.0).astype(BF16)
        out = out + rec[:, p_lane:p_lane + 1] * _dot(pick, ys)
    o_ref[...] = out


def _combine(tables, x2, route, y):
    t = x2.shape[0]
    tm = MOE_CHUNK
    nc = t // tm
    tok = lambda w: pl.BlockSpec((tm, w), lambda c: (c, 0))
    tab = lambda f: pl.BlockSpec((1, 3, N_EXPERTS), lambda c: (f(c), 0, 0), memory_space=pltpu.SMEM)
    return pl.pallas_call(
        _combine_kernel,
        grid=(nc,),
        in_specs=[tab(lambda c: c), tab(lambda c: jnp.minimum(c + 1, nc - 1)),
                  tok(D_MODEL), tok(LANES), pl.BlockSpec(memory_space=pl.ANY)],
        out_specs=tok(D_MODEL),
        out_shape=jax.ShapeDtypeStruct((t, D_MODEL), F32),
        scratch_shapes=[pltpu.VMEM((2, MOE_ROWS, PACKED), jnp.uint32), pltpu.SemaphoreType.DMA((2,))],
        compiler_params=_cparams(1),
        name="moe_combine",
    )(tables, tables, x2, route, y)


def _ffn_moe(x2, fn, router_w, w_gu, w_down):
    t = x2.shape[0]
    route, cnt, xs = _router(x2, fn, router_w)
    counts = cnt[:, 0, :N_EXPERTS].astype(jnp.int32)
    upto = jnp.cumsum(counts, axis=0)
    experts = jnp.arange(N_EXPERTS, dtype=jnp.int32)
    region = _expert_region(t)
    tables = jnp.stack([counts, jnp.cumsum(counts, axis=1) - counts,
                        upto - counts + experts[None, :] * region], axis=1)
    tiles = (upto[-1] + MOE_TILE - 1) // MOE_TILE
    tile_end = jnp.cumsum(tiles)
    max_rows = 2 * t + (t // MOE_CHUNK) * N_EXPERTS * (RUN_ALIGN - 1)
    n_steps = max_rows // MOE_TILE + N_EXPERTS
    j = jnp.arange(n_steps, dtype=jnp.int32)
    tile_expert = jnp.minimum(jnp.sum((j[:, None] >= tile_end[None, :]).astype(jnp.int32), axis=1), N_EXPERTS - 1)
    first_tile = jnp.sum(jnp.where(experts[None, :] == tile_expert[:, None], (tile_end - tiles)[None, :], 0), axis=1)
    tile_valid = (j < tile_end[-1]).astype(jnp.int32)
    spare = N_EXPERTS * region // MOE_TILE
    tile_block = jnp.where(tile_valid != 0, tile_expert * (region // MOE_TILE) + (j - first_tile), spare)
    y = _experts(tile_block, tile_expert, tile_valid, xs, w_gu, w_down)
    return _combine(tables, x2, route, y)


def _slot_cols(w, dims):
    k = w.shape[0]
    w = w.reshape(k, N_HEADS, dims)
    return jnp.pad(w, ((0, 0), (0, 0), (0, SLOT - dims))).reshape(k, HW)


def _slot_cols_v(w):
    k = w.shape[0]
    w = w.reshape(k, N_HEADS // 2, 2, B_V_DIM)
    zero = jnp.zeros((k, N_HEADS // 2, B_V_DIM), w.dtype)
    even = jnp.concatenate([w[:, :, 0], zero], axis=-1)
    odd = jnp.concatenate([zero, w[:, :, 1]], axis=-1)
    return jnp.stack([even, odd], axis=2).reshape(k, HW)


def _slot_vec(g, dims):
    return jnp.tile(jnp.pad(g, (0, SLOT - dims)), N_HEADS).reshape(1, HW)


def _rope_table(seq, rot_dim, lane0):
    half = rot_dim // 2
    lane = np.arange(SLOT)
    rot = (lane >= lane0) & (lane < lane0 + rot_dim)
    pair = np.where(rot, (lane - lane0) % half, 0)
    inv = ROPE_THETA ** (-jnp.arange(0, rot_dim, 2, dtype=F32) / rot_dim)
    inv = jnp.where(rot, inv[pair], 0.0)
    ang = jnp.arange(seq, dtype=F32)[:, None] * inv[None, :]
    upper = (rot & (lane - lane0 >= half)).astype(np.float32)
    lower = (rot & (lane - lane0 < half)).astype(np.float32)
    sin = jnp.sin(ang)
    return jnp.stack([jnp.cos(ang), sin * upper, -sin * lower])


def _constants():
    r = np.arange(MXU_DIM)
    seg = (r[:, None] // SLOT == r[None, :] // SLOT).astype(np.float32)
    place = np.zeros((N_HEADS * MAX_MOBA_BLOCKS, HW), np.float32)
    for h in range(N_HEADS):
        for n in range(MAX_MOBA_BLOCKS):
            place[h * MAX_MOBA_BLOCKS + n, h * SLOT + BIAS_LANE0 + n] = 1.0
    return jnp.asarray(seg, BF16), jnp.asarray(place, BF16)


def _layer_params(l, attn_norm, w_in, moba_q_norm, moba_k_norm, mla_cq_norm, w_uq, mla_ckv_norm,
                  w_ukv, mla_q_norm, mla_k_norm, w_branch_a, w_branch_b, w_out):
    wi = w_in[l]
    aw = N_HEADS * A_HEAD_DIM
    o_cq = 3 * aw
    o_ckv = o_cq + B_Q_RANK
    o_kr = o_ckv + B_KV_RANK
    o_g = o_kr + B_ROPE_DIM
    w_proj = jnp.concatenate([
        _slot_cols(wi[:, 0:aw], A_HEAD_DIM), _slot_cols(wi[:, aw:2 * aw], A_HEAD_DIM),
        _slot_cols_v(wi[:, 2 * aw:3 * aw]),
        wi[:, o_cq:o_ckv], wi[:, o_ckv:o_kr], jnp.pad(wi[:, o_kr:o_g], ((0, 0), (0, LANES - B_ROPE_DIM)))], axis=1)
    ukv = w_ukv[l].reshape(B_KV_RANK, N_HEADS, B_NOPE_DIM + B_V_DIM)
    w_ukv_s = jnp.concatenate([_slot_cols(ukv[:, :, :B_NOPE_DIM].reshape(B_KV_RANK, -1), B_NOPE_DIM),
                               _slot_cols_v(ukv[:, :, B_NOPE_DIM:].reshape(B_KV_RANK, -1))], axis=1)
    seg, place = _constants()
    return {
        "attn_norm": attn_norm[l].reshape(1, D_MODEL),
        "w_proj": w_proj.astype(BF16),
        "w_gates": wi[:, o_g:].astype(BF16),
        "w_uq": _slot_cols(w_uq[l], B_QK_DIM).astype(BF16),
        "w_ukv": w_ukv_s.astype(BF16),
        "cq_norm": mla_cq_norm[l].reshape(1, B_Q_RANK),
        "ckv_norm": mla_ckv_norm[l].reshape(1, B_KV_RANK),
        "qn_a": _slot_vec(moba_q_norm[l], A_HEAD_DIM),
        "kn_a": _slot_vec(moba_k_norm[l], A_HEAD_DIM),
        "qn_b": _slot_vec(mla_q_norm[l], B_QK_DIM),
        "kn_b": _slot_vec(mla_k_norm[l], B_QK_DIM),
        "seg": seg,
        "place": place,
        "w_br_a": w_branch_a[l].astype(BF16),
        "w_br_b": w_branch_b[l].astype(BF16),
        "w_out": w_out[l].astype(BF16),
    }


def kernel(x, attn_norm, w_in, moba_q_norm, moba_k_norm, mla_cq_norm, w_uq, mla_ckv_norm, w_ukv,
           mla_q_norm, mla_k_norm, w_branch_a, w_branch_b, w_out, ffn_norm, dense_w_gate_up,
           dense_w_down, router_w, expert_w_gate_up, expert_w_down):
    b, s, d = x.shape
    depth = attn_norm.shape[0]
    rope_a = _rope_table(s, A_ROT_DIM, 0)
    rope_b = _rope_table(s, B_ROPE_DIM, B_NOPE_DIM)
    for l in range(depth):
        p = _layer_params(l, attn_norm, w_in, moba_q_norm, moba_k_norm, mla_cq_norm, w_uq,
                          mla_ckv_norm, w_ukv, mla_q_norm, mla_k_norm, w_branch_a, w_branch_b, w_out)
        qa, ka, va, qb, kb, vb = _prologue(x, p, rope_a, rope_b)
        ya = _attention(qa, ka, va)
        yb = _attention(qb, kb, vb)
        x2 = _merge(x.reshape(b * s, d), p, ya.reshape(b * s, V_WIDTH), yb.reshape(b * s, V_WIDTH))
        fn = ffn_norm[l].reshape(1, D_MODEL)
        if l % 2 == 0:
            x2 = _ffn_dense(x2, fn, dense_w_gate_up[l // 2].astype(BF16), dense_w_down[l // 2].astype(BF16))
        else:
            rw = jnp.pad(router_w[l // 2], ((0, 0), (0, LANES - N_EXPERTS)))
            x2 = _ffn_moe(x2, fn, rw, expert_w_gate_up[l // 2].astype(BF16), expert_w_down[l // 2].astype(BF16))
        x = x2.reshape(b, s, d)
    return x
```

```python
import math

import jax
import jax.numpy as jnp
import numpy as np
from jax import lax
from jax.experimental import pallas as pl
from jax.experimental.pallas import tpu as pltpu

F32 = jnp.float32
BF16 = jnp.bfloat16

D_MODEL = 1024
N_HEADS = 8
A_HEAD_DIM = 64
A_ROT_DIM = 16
MOBA_BLOCK = 256
MOBA_TOPK = 3
MAX_MOBA_BLOCKS = 16
B_NOPE_DIM = 64
B_ROPE_DIM = 32
B_QK_DIM = B_NOPE_DIM + B_ROPE_DIM
B_V_DIM = 64
B_Q_RANK = 384
B_KV_RANK = 256
ROPE_THETA = 500000.0
EPS = 1e-6
NEG = -1e30
D_FF = 2816
N_EXPERTS = 8

LANES = 128
SLOT = LANES
MXU_DIM = 256
HW = N_HEADS * SLOT
BIAS_LANE0 = A_HEAD_DIM
V_WIDTH = N_HEADS * B_V_DIM

C_QA, C_KA, C_VA = 0, HW, 2 * HW
C_CQ = C_VA + HW
C_CKV = C_CQ + B_Q_RANK
C_KR = C_CKV + B_KV_RANK
N_PROJ = C_KR + LANES
V_LANE0 = (0, B_V_DIM)
SUM_LANE = (B_V_DIM, 0)
LOG2E = math.log2(math.e)

VMEM_LIMIT = 56 * 1024 * 1024


def _cparams(n_axes):
    return pltpu.CompilerParams(dimension_semantics=("arbitrary",) * n_axes,
                                vmem_limit_bytes=VMEM_LIMIT)


def _dot(a, b):
    return jnp.dot(a, b, preferred_element_type=F32)


def _dot_nt(a, b):
    return lax.dot_general(a, b, (((1,), (1,)), ((), ())), preferred_element_type=F32)


def _split_bf16(a):
    hi = a.astype(BF16)
    lo = (a - hi.astype(F32)).astype(BF16)
    return hi, lo


def _dot3(a, b):
    ah, al = _split_bf16(a)
    bh, bl = _split_bf16(b)
    return _dot(ah, bh) + (_dot(ah, bl) + _dot(al, bh))


def _dot3_nt(a, b):
    ah, al = _split_bf16(a)
    bh, bl = _split_bf16(b)
    return _dot_nt(ah, bh) + (_dot_nt(ah, bl) + _dot_nt(al, bh))


def _rms_rows(xf, g):
    ms = jnp.mean(xf * xf, axis=-1, keepdims=True)
    return xf * lax.rsqrt(ms + EPS) * g


def _sigmoid(z):
    return 1.0 / (1.0 + jnp.exp(-z))


PROLOGUE_BLOCKS = 2


def _slot_norm_rope(x, g_ref, seg_ref, inv_n, rope, shift):
    cos, sin_fwd, sin_bwd = rope
    slots = []
    for c in range(HW // MXU_DIM):
        xc = x[:, c * MXU_DIM:(c + 1) * MXU_DIM]
        ss = _dot((xc * xc).astype(BF16), seg_ref[...])
        xn = xc * lax.rsqrt(ss * inv_n + EPS) * g_ref[:, c * MXU_DIM:(c + 1) * MXU_DIM]
        for j in range(MXU_DIM // SLOT):
            xs = xn[:, j * SLOT:(j + 1) * SLOT]
            slots.append(xs * cos + pltpu.roll(xs, shift, 1) * sin_fwd
                         + pltpu.roll(xs, SLOT - shift, 1) * sin_bwd)
    return slots


def _prologue_kernel(x_ref, an_ref, w_ref, wuq_ref, wukv_ref, cqn_ref, ckvn_ref,
                     qna_ref, kna_ref, qnb_ref, knb_ref, seg_ref, place_ref,
                     ropea_ref, ropeb_ref,
                     qa_ref, ka_ref, va_ref, qb_ref, kb_ref, vb_ref, kmean_scr):
    @pl.when(pl.program_id(1) == 0)
    def _():
        kmean_scr[...] = jnp.zeros_like(kmean_scr)

    out_refs = (qa_ref, ka_ref, va_ref, qb_ref, kb_ref, vb_ref)
    for j in range(x_ref.shape[1] // MOBA_BLOCK):
        rows = slice(j * MOBA_BLOCK, (j + 1) * MOBA_BLOCK)
        hn = _rms_rows(x_ref[0, rows], an_ref[...]).astype(BF16)
        proj = _dot(hn, w_ref[...])
        _prologue_block(proj, pl.program_id(1) * (x_ref.shape[1] // MOBA_BLOCK) + j, rows,
                        wuq_ref, wukv_ref, cqn_ref, ckvn_ref, qna_ref, kna_ref, qnb_ref, knb_ref,
                        seg_ref, place_ref, [ropea_ref[t, rows] for t in range(3)],
                        [ropeb_ref[t, rows] for t in range(3)], out_refs, kmean_scr)


def _prologue_block(proj, s, rows, wuq_ref, wukv_ref, cqn_ref, ckvn_ref, qna_ref, kna_ref, qnb_ref, knb_ref,
                    seg_ref, place_ref, rope_a, rope_b, out_refs, kmean_scr):
    qa_ref, ka_ref, va_ref, qb_ref, kb_ref, vb_ref = out_refs
    tm = MOBA_BLOCK

    qa = _slot_norm_rope(proj[:, C_QA:C_QA + HW], qna_ref, seg_ref, 1.0 / A_HEAD_DIM, rope_a, A_ROT_DIM // 2)
    ka = _slot_norm_rope(proj[:, C_KA:C_KA + HW], kna_ref, seg_ref, 1.0 / A_HEAD_DIM, rope_a, A_ROT_DIM // 2)

    kmean = kmean_scr[...]
    gates = [_dot3_nt(kmean[:, h * SLOT:(h + 1) * SLOT], qa[h]) for h in range(N_HEADS)]
    g3 = jnp.stack(gates, axis=0)
    blk = lax.broadcasted_iota(jnp.int32, g3.shape, 1)
    g3 = jnp.where(blk < s, g3, -jnp.inf)
    rank3 = jnp.zeros(g3.shape, F32)
    for n in range(MAX_MOBA_BLOCKS):
        row = g3[:, n:n + 1, :]
        rank3 = rank3 + jnp.where((row > g3) | ((row == g3) & (blk > n)), 1.0, 0.0)
    sel3 = jnp.where(((rank3 < MOBA_TOPK) & (blk < s)) | (blk == s), 1.0, 0.0)
    sel_t = sel3.reshape(N_HEADS * MAX_MOBA_BLOCKS, tm)
    placed = _dot(sel_t.T.astype(BF16), place_ref[...])
    lane = lax.broadcasted_iota(jnp.int32, (tm, SLOT), 1)
    is_bias = (lane >= BIAS_LANE0) & (lane < BIAS_LANE0 + MAX_MOBA_BLOCKS)
    scale_a = LOG2E / math.sqrt(A_HEAD_DIM)
    for h in range(N_HEADS):
        sl = slice(h * SLOT, (h + 1) * SLOT)
        bias = (placed[:, sl] - 1.0) * (-NEG)
        qa_ref[0, rows, sl] = jnp.where(is_bias, bias, qa[h] * scale_a).astype(BF16)
        ka_ref[0, rows, sl] = jnp.where(lane == BIAS_LANE0 + s, 1.0, ka[h]).astype(BF16)
        va_ref[0, rows, sl] = jnp.where(lane == SUM_LANE[h % 2], 1.0,
                                        proj[:, C_VA + h * SLOT:C_VA + (h + 1) * SLOT]).astype(BF16)

    km = jnp.concatenate([jnp.mean(k, axis=0, keepdims=True) for k in ka], axis=1)
    kmean_scr[pl.ds(s, 1), :] = km

    cq = _rms_rows(proj[:, C_CQ:C_CQ + B_Q_RANK], cqn_ref[...]).astype(BF16)
    qb = _dot(cq, wuq_ref[...])
    ckv = _rms_rows(proj[:, C_CKV:C_CKV + B_KV_RANK], ckvn_ref[...]).astype(BF16)
    kv = _dot(ckv, wukv_ref[...])
    kr = pltpu.roll(proj[:, C_KR:C_KR + LANES], B_NOPE_DIM, 1)
    kb = kv[:, :HW] + jnp.concatenate([kr] * N_HEADS, axis=1)
    scale_b = LOG2E / math.sqrt(B_QK_DIM)
    qb_s = _slot_norm_rope(qb, qnb_ref, seg_ref, 1.0 / B_QK_DIM, rope_b, B_ROPE_DIM // 2)
    kb_s = _slot_norm_rope(kb, knb_ref, seg_ref, 1.0 / B_QK_DIM, rope_b, B_ROPE_DIM // 2)
    for h in range(N_HEADS):
        sl = slice(h * SLOT, (h + 1) * SLOT)
        qb_ref[0, rows, sl] = (qb_s[h] * scale_b).astype(BF16)
        kb_ref[0, rows, sl] = kb_s[h].astype(BF16)
        vb_ref[0, rows, sl] = jnp.where(lane == SUM_LANE[h % 2], 1.0,
                                        kv[:, HW + h * SLOT:HW + (h + 1) * SLOT]).astype(BF16)


def _prologue(x, p, rope_a, rope_b):
    b, s, _ = x.shape
    tm = PROLOGUE_BLOCKS * MOBA_BLOCK
    assert s % tm == 0 and s // MOBA_BLOCK <= MAX_MOBA_BLOCKS
    full = lambda a: pl.BlockSpec(a.shape, lambda i, j: (0,) * a.ndim)
    tok = lambda w: pl.BlockSpec((1, tm, w), lambda i, j: (i, j, 0))
    rope = pl.BlockSpec((3, tm, LANES), lambda i, j: (0, j, 0))
    consts = [p["attn_norm"], p["w_proj"], p["w_uq"], p["w_ukv"], p["cq_norm"], p["ckv_norm"],
              p["qn_a"], p["kn_a"], p["qn_b"], p["kn_b"], p["seg"], p["place"]]
    out_shape = [jax.ShapeDtypeStruct((b, s, HW), BF16)] * 6
    return pl.pallas_call(
        _prologue_kernel,
        grid=(b, s // tm),
        in_specs=[tok(D_MODEL)] + [full(a) for a in consts] + [rope, rope],
        out_specs=[tok(HW)] * 6,
        out_shape=out_shape,
        scratch_shapes=[pltpu.VMEM((MAX_MOBA_BLOCKS, HW), F32)],
        compiler_params=_cparams(2),
        name="prologue",
    )(x, *consts, rope_a, rope_b)


ATTN_TILE = 1024
ATTN_HEADS = 4
DIAG_SPLIT = 2


def _attn_kernel(q_ref, k_ref, v_ref, o_ref, m_scr, acc_scr):
    i = pl.program_id(2)
    t = q_ref.shape[1]
    tr = t // DIAG_SPLIT

    def scores(n, hh):
        sl = slice(hh * SLOT, (hh + 1) * SLOT)
        off = pl.multiple_of(n * t, t)
        return _dot_nt(q_ref[0, :, sl], k_ref[0, pl.ds(off, t), sl]), v_ref[0, pl.ds(off, t), sl]

    diag = pl.multiple_of(i * t, t)
    for hh in range(ATTN_HEADS):
        sl = slice(hh * SLOT, (hh + 1) * SLOT)
        for r in range(DIAG_SPLIT):
            rows = slice(r * tr, (r + 1) * tr)
            nk = (r + 1) * tr
            sc = _dot_nt(q_ref[0, rows, sl], k_ref[0, pl.ds(diag, nk), sl])
            row = lax.broadcasted_iota(jnp.int32, (tr, nk), 0)
            col = lax.broadcasted_iota(jnp.int32, (tr, nk), 1)
            sc = jnp.where(col <= row + r * tr, sc, NEG)
            m = jnp.max(sc, axis=1, keepdims=True)
            m_scr[hh, rows] = m
            acc_scr[hh, rows] = _dot(jnp.exp2(sc - m).astype(BF16), v_ref[0, pl.ds(diag, nk), sl])
    first = [(jnp.max(jnp.broadcast_to(m_scr[hh], (t, LANES)), axis=1, keepdims=True), acc_scr[hh])
             for hh in range(ATTN_HEADS)]

    def past_tile(n, state):
        new = []
        for hh, (m, acc) in enumerate(state):
            sc, v = scores(n, hh)
            m_new = jnp.maximum(m, jnp.max(sc, axis=1, keepdims=True))
            new.append((m_new, jnp.exp2(m - m_new) * acc + _dot(jnp.exp2(sc - m_new).astype(BF16), v)))
        return tuple(new)

    last = lax.fori_loop(0, i, past_tile, tuple(first))
    outs = [acc / acc[:, SUM_LANE[hh % 2]:SUM_LANE[hh % 2] + 1] for hh, (_, acc) in enumerate(last)]
    lane = lax.broadcasted_iota(jnp.int32, (t, LANES), 1)
    for p in range(ATTN_HEADS // 2):
        pair = jnp.where(lane < B_V_DIM, outs[2 * p], outs[2 * p + 1])
        o_ref[0, :, p * LANES:(p + 1) * LANES] = pair.astype(o_ref.dtype)


def _attention(q, k, v):
    b, s, _ = q.shape
    t = min(ATTN_TILE, s)
    assert s % t == 0 and t % MOBA_BLOCK == 0
    return pl.pallas_call(
        _attn_kernel,
        grid=(b, N_HEADS // ATTN_HEADS, s // t),
        in_specs=[pl.BlockSpec((1, t, ATTN_HEADS * SLOT), lambda bi, p, i: (bi, i, p)),
                  pl.BlockSpec((1, s, ATTN_HEADS * SLOT), lambda bi, p, i: (bi, 0, p)),
                  pl.BlockSpec((1, s, ATTN_HEADS * SLOT), lambda bi, p, i: (bi, 0, p))],
        out_specs=pl.BlockSpec((1, t, ATTN_HEADS * B_V_DIM), lambda bi, p, i: (bi, i, p)),
        out_shape=jax.ShapeDtypeStruct((b, s, V_WIDTH), BF16),
        scratch_shapes=[pltpu.VMEM((ATTN_HEADS, t, 1), F32), pltpu.VMEM((ATTN_HEADS, t, LANES), F32)],
        compiler_params=_cparams(3),
        name="attention",
    )(q, k, v)


def _merge_kernel(x_ref, an_ref, ya_ref, yb_ref, wg_ref, wa_ref, wb_ref, wo_ref, o_ref):
    xf = x_ref[...]
    hn = _rms_rows(xf, an_ref[...]).astype(BF16)
    gates = _dot(hn, wg_ref[...])
    pa = _dot(ya_ref[...], wa_ref[...])
    pb = _dot(yb_ref[...], wb_ref[...])
    merged = _sigmoid(gates[:, :D_MODEL]) * pa + _sigmoid(gates[:, D_MODEL:]) * pb
    o_ref[...] = xf + _dot(merged.astype(BF16), wo_ref[...])


def _merge(x2, p, ya2, yb2, tm=512):
    t = x2.shape[0]
    full = lambda a: pl.BlockSpec(a.shape, lambda i: (0,) * a.ndim)
    tok = lambda w: pl.BlockSpec((tm, w), lambda i: (i, 0))
    consts = [p["w_gates"], p["w_br_a"], p["w_br_b"], p["w_out"]]
    return pl.pallas_call(
        _merge_kernel,
        grid=(t // tm,),
        in_specs=[tok(D_MODEL), full(p["attn_norm"]), tok(V_WIDTH), tok(V_WIDTH)] + [full(a) for a in consts],
        out_specs=tok(D_MODEL),
        out_shape=jax.ShapeDtypeStruct((t, D_MODEL), F32),
        compiler_params=_cparams(1),
        name="merge",
    )(x2, p["attn_norm"], ya2, yb2, *consts)


FF_CHUNK = MXU_DIM


def _ffn_kernel(x_ref, fn_ref, wgu_ref, wd_ref, o_ref, act_scr):
    xf = x_ref[...]
    h = _rms_rows(xf, fn_ref[...]).astype(BF16)
    for c in range(D_FF // FF_CHUNK):
        g = _dot(h, wgu_ref[:, c * FF_CHUNK:(c + 1) * FF_CHUNK])
        u = _dot(h, wgu_ref[:, D_FF + c * FF_CHUNK:D_FF + (c + 1) * FF_CHUNK])
        act_scr[:, c * FF_CHUNK:(c + 1) * FF_CHUNK] = (g * _sigmoid(g) * u).astype(BF16)
    o_ref[...] = xf + _dot(act_scr[...], wd_ref[...])


def _ffn_dense(x2, fn, w_gu, w_down, tm=512):
    t = x2.shape[0]
    full = lambda a: pl.BlockSpec(a.shape, lambda i: (0,) * a.ndim)
    tok = pl.BlockSpec((tm, D_MODEL), lambda i: (i, 0))
    return pl.pallas_call(
        _ffn_kernel,
        grid=(t // tm,),
        in_specs=[tok, full(fn), full(w_gu), full(w_down)],
        out_specs=tok,
        out_shape=jax.ShapeDtypeStruct((t, D_MODEL), F32),
        scratch_shapes=[pltpu.VMEM((tm, D_FF), BF16)],
        compiler_params=_cparams(1),
        name="ffn_dense",
    )(x2, fn, w_gu, w_down)


MOE_TILE = 512
MOE_CHUNK = 512
RUN_ALIGN = 16
MOE_ROWS = -(-(2 * MOE_CHUNK + N_EXPERTS * RUN_ALIGN) // LANES) * LANES
RUN_BITS = range(RUN_ALIGN.bit_length() - 1, (2 * MOE_CHUNK).bit_length())
R_POS1, R_POS2, R_P1, R_P2 = range(4)
S_CNT, S_LOCAL, S_GLOBAL = range(3)


def _run_rows(first, size):
    return pl.ds(pl.multiple_of(first, RUN_ALIGN), size)


def _run_copies(table, src_of, dst_of, sem, start):
    for e in range(N_EXPERTS):
        n = table[S_CNT, e]
        for bit in RUN_BITS:
            size = 1 << bit
            above = n & ~(2 * size - 1)

            @pl.when((n & size) != 0)
            def _(e=e, size=size, above=above):
                cp = pltpu.make_async_copy(src_of(e, above, size), dst_of(e, above, size), sem)
                if start:
                    cp.start(priority=e % 2)
                else:
                    cp.wait()


def _router_kernel(x_ref, fn_ref, rw_ref, route_ref, cnt_ref, xs_ref,
                   carry_scr, stage, tab_v, tab_s, zeros_v, sem, tab_sem):
    c = pl.program_id(0)
    last = pl.num_programs(0) - 1
    tm = x_ref.shape[0]
    slot = c % 2
    cap = (xs_ref.shape[0] - MOE_TILE) // N_EXPERTS

    @pl.when(c == 0)
    def _():
        carry_scr[...] = jnp.zeros_like(carry_scr)

    hf = _rms_rows(x_ref[...], fn_ref[...])
    lane = lax.broadcasted_iota(jnp.int32, (tm, LANES), 1)
    logits = jnp.where(lane < N_EXPERTS, _dot3(hf, rw_ref[...]), -jnp.inf)
    m1 = jnp.max(logits, axis=1, keepdims=True)
    i1 = jnp.min(jnp.where(logits == m1, lane, LANES), axis=1, keepdims=True)
    rest = jnp.where(lane == i1, -jnp.inf, logits)
    m2 = jnp.max(rest, axis=1, keepdims=True)
    i2 = jnp.min(jnp.where(rest == m2, lane, LANES), axis=1, keepdims=True)
    e2 = jnp.exp(m2 - m1)
    p1 = 1.0 / (1.0 + e2)
    p2 = e2 / (1.0 + e2)

    onehot = jnp.where((lane == i1) | (lane == i2), 1.0, 0.0)
    r = lax.broadcasted_iota(jnp.int32, (tm, tm), 0)
    cc = lax.broadcasted_iota(jnp.int32, (tm, tm), 1)
    earlier = jnp.where(cc < r, 1.0, 0.0).astype(BF16)
    before = _dot(earlier, onehot.astype(BF16))
    er = lax.broadcasted_iota(jnp.int32, (LANES, LANES), 0)
    ec = lax.broadcasted_iota(jnp.int32, (LANES, LANES), 1)
    cnt = jnp.floor((jnp.sum(onehot, axis=0, keepdims=True) + (RUN_ALIGN - 1)) * (1.0 / RUN_ALIGN)) * RUN_ALIGN
    local = _dot(jnp.broadcast_to(cnt, (8, LANES)).astype(BF16), jnp.where(er < ec, 1.0, 0.0).astype(BF16))[0:1]
    pos = before + local
    pos1 = jnp.sum(jnp.where(lane == i1, pos, 0.0), axis=1, keepdims=True)
    pos2 = jnp.sum(jnp.where(lane == i2, pos, 0.0), axis=1, keepdims=True)
    rec = jnp.zeros((tm, LANES), F32)
    for where_, val in ((R_POS1, pos1), (R_POS2, pos2), (R_P1, p1), (R_P2, p2)):
        rec = jnp.where(lane == where_, val, rec)
    route_ref[...] = rec

    slot_lane = lax.broadcasted_iota(jnp.int32, (tm, MOE_ROWS), 1).astype(F32)
    pick_t = jnp.where((slot_lane == pos1) | (slot_lane == pos2), 1.0, 0.0)
    sorted_rows = _dot(pick_t.T.astype(BF16), hf.astype(BF16))

    stage[slot] = sorted_rows.astype(BF16)

    carry = carry_scr[0:1, :]
    srow = lax.broadcasted_iota(jnp.int32, (8, LANES), 0)
    elane = lax.broadcasted_iota(jnp.int32, (8, LANES), 1)
    table = jnp.where(srow == S_CNT, cnt, jnp.where(srow == S_LOCAL, local, carry + (elane * cap).astype(F32)))
    tab_v[...] = table.astype(jnp.int32)
    total = carry + cnt
    carry_scr[...] = jnp.broadcast_to(total, carry_scr.shape)
    cnt_ref[0] = jnp.broadcast_to(cnt, (8, LANES))

    tcp = pltpu.make_async_copy(tab_v, tab_s.at[slot], tab_sem)
    tcp.start()
    tcp.wait()

    def runs(s, start):
        _run_copies(tab_s.at[s],
                    lambda e, off, size: stage.at[s, _run_rows(tab_s[s, S_LOCAL, e] + off, size)],
                    lambda e, off, size: xs_ref.at[_run_rows(tab_s[s, S_GLOBAL, e] + off, size)],
                    sem.at[s], start)

    runs(slot, True)

    @pl.when(c > 0)
    def _():
        runs(1 - slot, False)

    @pl.when(c == last)
    def _():
        runs(slot, False)
        zeros_v[...] = jnp.zeros_like(zeros_v)
        pad = (-total.astype(jnp.int32)) & (MOE_TILE - 1)
        tab_v[...] = jnp.where(srow == S_CNT, pad, jnp.where(srow == S_LOCAL, 0,
                                                               total.astype(jnp.int32) + elane * cap))
        pcp = pltpu.make_async_copy(tab_v, tab_s.at[slot], tab_sem)
        pcp.start()
        pcp.wait()
        for start in (True, False):
            _run_copies(tab_s.at[slot],
                        lambda e, off, size: zeros_v.at[_run_rows(off, size)],
                        lambda e, off, size: xs_ref.at[_run_rows(tab_s[slot, S_GLOBAL, e] + off, size)],
                        sem.at[slot], start)


def _expert_region(t):
    worst = t + (t // MOE_CHUNK) * (RUN_ALIGN - 1)
    return -(-worst // MOE_TILE) * MOE_TILE


def _router(x2, fn, router_w):
    t = x2.shape[0]
    tm = MOE_CHUNK
    nc = t // tm
    n_rows = N_EXPERTS * _expert_region(t) + MOE_TILE
    full = lambda a: pl.BlockSpec(a.shape, lambda i: (0,) * a.ndim)
    return pl.pallas_call(
        _router_kernel,
        grid=(nc,),
        in_specs=[pl.BlockSpec((tm, D_MODEL), lambda i: (i, 0)), full(fn), full(router_w)],
        out_specs=[pl.BlockSpec((tm, LANES), lambda i: (i, 0)), pl.BlockSpec((1, 8, LANES), lambda i: (i, 0, 0)),
                   pl.BlockSpec(memory_space=pl.ANY)],
        out_shape=[jax.ShapeDtypeStruct((t, LANES), F32), jax.ShapeDtypeStruct((nc, 8, LANES), F32),
                   jax.ShapeDtypeStruct((n_rows, D_MODEL), BF16)],
        scratch_shapes=[pltpu.VMEM((8, LANES), F32), pltpu.VMEM((2, MOE_ROWS, D_MODEL), BF16),
                        pltpu.VMEM((8, LANES), jnp.int32), pltpu.SMEM((2, 8, LANES), jnp.int32),
                        pltpu.VMEM((MOE_TILE, D_MODEL), BF16),
                        pltpu.SemaphoreType.DMA((2,)), pltpu.SemaphoreType.DMA(())],
        compiler_params=_cparams(1),
        name="moe_router",
    )(x2, fn, router_w)


def _experts_kernel(tb_ref, te_ref, tv_ref, xs_ref, wgu_ref, wd_ref, y_ref, act_scr):
    del tb_ref, te_ref
    j = pl.program_id(0)

    @pl.when(tv_ref[j] != 0)
    def _():
        h = xs_ref[...]
        for c in range(D_FF // FF_CHUNK):
            g = _dot(h, wgu_ref[0, :, c * FF_CHUNK:(c + 1) * FF_CHUNK])
            u = _dot(h, wgu_ref[0, :, D_FF + c * FF_CHUNK:D_FF + (c + 1) * FF_CHUNK])
            act_scr[:, c * FF_CHUNK:(c + 1) * FF_CHUNK] = (g * _sigmoid(g) * u).astype(BF16)
        y_ref[...] = _dot(act_scr[...], wd_ref[0]).astype(BF16)

    @pl.when(tv_ref[j] == 0)
    def _():
        y_ref[...] = jnp.zeros_like(y_ref)


def _experts(tile_block, tile_expert, tile_valid, xs, w_gu, w_down):
    tm = MOE_TILE
    row = pl.BlockSpec((tm, D_MODEL), lambda j, tb, te, tv: (tb[j], 0))
    return pl.pallas_call(
        _experts_kernel,
        grid_spec=pltpu.PrefetchScalarGridSpec(
            num_scalar_prefetch=3,
            grid=(tile_block.shape[0],),
            in_specs=[row,
                      pl.BlockSpec((1, D_MODEL, 2 * D_FF), lambda j, tb, te, tv: (te[j], 0, 0)),
                      pl.BlockSpec((1, D_FF, D_MODEL), lambda j, tb, te, tv: (te[j], 0, 0))],
            out_specs=row,
            scratch_shapes=[pltpu.VMEM((tm, D_FF), BF16)]),
        out_shape=jax.ShapeDtypeStruct(xs.shape, BF16),
        compiler_params=_cparams(1),
        name="moe_experts",
    )(tile_block, tile_expert, tile_valid, xs, w_gu, w_down)


def _combine_kernel(tcur_ref, tnext_ref, x_ref, route_ref, y_ref, o_ref, ybuf, sem):
    c = pl.program_id(0)
    last = pl.num_programs(0) - 1
    tm = x_ref.shape[0]
    slot = c % 2

    def runs(table, s, start):
        _run_copies(table.at[0],
                    lambda e, off, size: y_ref.at[_run_rows(table[0, S_GLOBAL, e] + off, size)],
                    lambda e, off, size: ybuf.at[s, _run_rows(table[0, S_LOCAL, e] + off, size)],
                    sem.at[s], start)

    @pl.when(c == 0)
    def _():
        ybuf[...] = jnp.zeros_like(ybuf)
        runs(tcur_ref, 0, True)

    @pl.when(c < last)
    def _():
        runs(tnext_ref, 1 - slot, True)

    runs(tcur_ref, slot, False)

    rec = route_ref[...]
    ys = ybuf[slot]
    slot_lane = lax.broadcasted_iota(jnp.int32, (tm, MOE_ROWS), 1).astype(F32)
    out = x_ref[...]
    for pos_lane, p_lane in ((R_POS1, R_P1), (R_POS2, R_P2)):
        pick = jnp.where(slot_lane == rec[:, pos_lane:pos_lane + 1], 1.0, 0.0).astype(BF16)
        out = out + rec[:, p_lane:p_lane + 1] * _dot(pick, ys)
    o_ref[...] = out


def _combine(tables, x2, route, y):
    t = x2.shape[0]
    tm = MOE_CHUNK
    nc = t // tm
    tok = lambda w: pl.BlockSpec((tm, w), lambda c: (c, 0))
    tab = lambda f: pl.BlockSpec((1, 3, N_EXPERTS), lambda c: (f(c), 0, 0), memory_space=pltpu.SMEM)
    return pl.pallas_call(
        _combine_kernel,
        grid=(nc,),
        in_specs=[tab(lambda c: c), tab(lambda c: jnp.minimum(c + 1, nc - 1)),
                  tok(D_MODEL), tok(LANES), pl.BlockSpec(memory_space=pl.ANY)],
        out_specs=tok(D_MODEL),
        out_shape=jax.ShapeDtypeStruct((t, D_MODEL), F32),
        scratch_shapes=[pltpu.VMEM((2, MOE_ROWS, D_MODEL), BF16), pltpu.SemaphoreType.DMA((2,))],
        compiler_params=_cparams(1),
        name="moe_combine",
    )(tables, tables, x2, route, y)


def _ffn_moe(x2, fn, router_w, w_gu, w_down):
    t = x2.shape[0]
    route, cnt, xs = _router(x2, fn, router_w)
    counts = cnt[:, 0, :N_EXPERTS].astype(jnp.int32)
    upto = jnp.cumsum(counts, axis=0)
    experts = jnp.arange(N_EXPERTS, dtype=jnp.int32)
    region = _expert_region(t)
    tables = jnp.stack([counts, jnp.cumsum(counts, axis=1) - counts,
                        upto - counts + experts[None, :] * region], axis=1)
    tiles = (upto[-1] + MOE_TILE - 1) // MOE_TILE
    tile_end = jnp.cumsum(tiles)
    max_rows = 2 * t + (t // MOE_CHUNK) * N_EXPERTS * (RUN_ALIGN - 1)
    n_steps = max_rows // MOE_TILE + N_EXPERTS
    j = jnp.arange(n_steps, dtype=jnp.int32)
    tile_expert = jnp.minimum(jnp.sum((j[:, None] >= tile_end[None, :]).astype(jnp.int32), axis=1), N_EXPERTS - 1)
    first_tile = jnp.sum(jnp.where(experts[None, :] == tile_expert[:, None], (tile_end - tiles)[None, :], 0), axis=1)
    tile_valid = (j < tile_end[-1]).astype(jnp.int32)
    spare = N_EXPERTS * region // MOE_TILE
    tile_block = jnp.where(tile_valid != 0, tile_expert * (region // MOE_TILE) + (j - first_tile), spare)
    y = _experts(tile_block, tile_expert, tile_valid, xs, w_gu, w_down)
    return _combine(tables, x2, route, y)


def _slot_cols(w, dims):
    k = w.shape[0]
    w = w.reshape(k, N_HEADS, dims)
    return jnp.pad(w, ((0, 0), (0, 0), (0, SLOT - dims))).reshape(k, HW)


def _slot_cols_v(w):
    k = w.shape[0]
    w = w.reshape(k, N_HEADS // 2, 2, B_V_DIM)
    zero = jnp.zeros((k, N_HEADS // 2, B_V_DIM), w.dtype)
    even = jnp.concatenate([w[:, :, 0], zero], axis=-1)
    odd = jnp.concatenate([zero, w[:, :, 1]], axis=-1)
    return jnp.stack([even, odd], axis=2).reshape(k, HW)


def _slot_vec(g, dims):
    return jnp.tile(jnp.pad(g, (0, SLOT - dims)), N_HEADS).reshape(1, HW)


def _rope_table(seq, rot_dim, lane0):
    half = rot_dim // 2
    lane = np.arange(SLOT)
    rot = (lane >= lane0) & (lane < lane0 + rot_dim)
    pair = np.where(rot, (lane - lane0) % half, 0)
    inv = ROPE_THETA ** (-jnp.arange(0, rot_dim, 2, dtype=F32) / rot_dim)
    inv = jnp.where(rot, inv[pair], 0.0)
    ang = jnp.arange(seq, dtype=F32)[:, None] * inv[None, :]
    upper = (rot & (lane - lane0 >= half)).astype(np.float32)
    lower = (rot & (lane - lane0 < half)).astype(np.float32)
    sin = jnp.sin(ang)
    return jnp.stack([jnp.cos(ang), sin * upper, -sin * lower])


def _constants():
    r = np.arange(MXU_DIM)
    seg = (r[:, None] // SLOT == r[None, :] // SLOT).astype(np.float32)
    place = np.zeros((N_HEADS * MAX_MOBA_BLOCKS, HW), np.float32)
    for h in range(N_HEADS):
        for n in range(MAX_MOBA_BLOCKS):
            place[h * MAX_MOBA_BLOCKS + n, h * SLOT + BIAS_LANE0 + n] = 1.0
    return jnp.asarray(seg, BF16), jnp.asarray(place, BF16)


def _layer_params(l, attn_norm, w_in, moba_q_norm, moba_k_norm, mla_cq_norm, w_uq, mla_ckv_norm,
                  w_ukv, mla_q_norm, mla_k_norm, w_branch_a, w_branch_b, w_out):
    wi = w_in[l]
    aw = N_HEADS * A_HEAD_DIM
    o_cq = 3 * aw
    o_ckv = o_cq + B_Q_RANK
    o_kr = o_ckv + B_KV_RANK
    o_g = o_kr + B_ROPE_DIM
    w_proj = jnp.concatenate([
        _slot_cols(wi[:, 0:aw], A_HEAD_DIM), _slot_cols(wi[:, aw:2 * aw], A_HEAD_DIM),
        _slot_cols_v(wi[:, 2 * aw:3 * aw]),
        wi[:, o_cq:o_ckv], wi[:, o_ckv:o_kr], jnp.pad(wi[:, o_kr:o_g], ((0, 0), (0, LANES - B_ROPE_DIM)))], axis=1)
    ukv = w_ukv[l].reshape(B_KV_RANK, N_HEADS, B_NOPE_DIM + B_V_DIM)
    w_ukv_s = jnp.concatenate([_slot_cols(ukv[:, :, :B_NOPE_DIM].reshape(B_KV_RANK, -1), B_NOPE_DIM),
                               _slot_cols_v(ukv[:, :, B_NOPE_DIM:].reshape(B_KV_RANK, -1))], axis=1)
    seg, place = _constants()
    return {
        "attn_norm": attn_norm[l].reshape(1, D_MODEL),
        "w_proj": w_proj.astype(BF16),
        "w_gates": wi[:, o_g:].astype(BF16),
        "w_uq": _slot_cols(w_uq[l], B_QK_DIM).astype(BF16),
        "w_ukv": w_ukv_s.astype(BF16),
        "cq_norm": mla_cq_norm[l].reshape(1, B_Q_RANK),
        "ckv_norm": mla_ckv_norm[l].reshape(1, B_KV_RANK),
        "qn_a": _slot_vec(moba_q_norm[l], A_HEAD_DIM),
        "kn_a": _slot_vec(moba_k_norm[l], A_HEAD_DIM),
        "qn_b": _slot_vec(mla_q_norm[l], B_QK_DIM),
        "kn_b": _slot_vec(mla_k_norm[l], B_QK_DIM),
        "seg": seg,
        "place": place,
        "w_br_a": w_branch_a[l].astype(BF16),
        "w_br_b": w_branch_b[l].astype(BF16),
        "w_out": w_out[l].astype(BF16),
    }


def kernel(x, attn_norm, w_in, moba_q_norm, moba_k_norm, mla_cq_norm, w_uq, mla_ckv_norm, w_ukv,
           mla_q_norm, mla_k_norm, w_branch_a, w_branch_b, w_out, ffn_norm, dense_w_gate_up,
           dense_w_down, router_w, expert_w_gate_up, expert_w_down):
    b, s, d = x.shape
    depth = attn_norm.shape[0]
    rope_a = _rope_table(s, A_ROT_DIM, 0)
    rope_b = _rope_table(s, B_ROPE_DIM, B_NOPE_DIM)
    for l in range(depth):
        p = _layer_params(l, attn_norm, w_in, moba_q_norm, moba_k_norm, mla_cq_norm, w_uq,
                          mla_ckv_norm, w_ukv, mla_q_norm, mla_k_norm, w_branch_a, w_branch_b, w_out)
        qa, ka, va, qb, kb, vb = _prologue(x, p, rope_a, rope_b)
        ya = _attention(qa, ka, va)
        yb = _attention(qb, kb, vb)
        x2 = _merge(x.reshape(b * s, d), p, ya.reshape(b * s, V_WIDTH), yb.reshape(b * s, V_WIDTH))
        fn = ffn_norm[l].reshape(1, D_MODEL)
        if l % 2 == 0:
            x2 = _ffn_dense(x2, fn, dense_w_gate_up[l // 2].astype(BF16), dense_w_down[l // 2].astype(BF16))
        else:
            rw = jnp.pad(router_w[l // 2], ((0, 0), (0, LANES - N_EXPERTS)))
            x2 = _ffn_moe(x2, fn, rw, expert_w_gate_up[l // 2].astype(BF16), expert_w_down[l // 2].astype(BF16))
        x = x2.reshape(b, s, d)
    return x
```

```python
import math

import jax
import jax.numpy as jnp
import numpy as np
from jax import lax
from jax.experimental import pallas as pl
from jax.experimental.pallas import tpu as pltpu

F32 = jnp.float32
BF16 = jnp.bfloat16

D_MODEL = 1024
N_HEADS = 8
A_HEAD_DIM = 64
A_ROT_DIM = 16
MOBA_BLOCK = 256
MOBA_TOPK = 3
MAX_MOBA_BLOCKS = 16
B_NOPE_DIM = 64
B_ROPE_DIM = 32
B_QK_DIM = B_NOPE_DIM + B_ROPE_DIM
B_V_DIM = 64
B_Q_RANK = 384
B_KV_RANK = 256
ROPE_THETA = 500000.0
EPS = 1e-6
NEG = -1e30
D_FF = 2816
N_EXPERTS = 8

LANES = 128
SLOT = LANES
MXU_DIM = 256
HW = N_HEADS * SLOT
BIAS_LANE0 = A_HEAD_DIM
V_WIDTH = N_HEADS * B_V_DIM

C_QA, C_KA, C_VA = 0, HW, 2 * HW
C_CQ = C_VA + HW
C_CKV = C_CQ + B_Q_RANK
C_KR = C_CKV + B_KV_RANK
N_PROJ = C_KR + LANES
V_LANE0 = (0, B_V_DIM)
SUM_LANE = (B_V_DIM, 0)
LOG2E = math.log2(math.e)

VMEM_LIMIT = 56 * 1024 * 1024


def _cparams(n_axes):
    return pltpu.CompilerParams(dimension_semantics=("arbitrary",) * n_axes,
                                vmem_limit_bytes=VMEM_LIMIT)


def _dot(a, b):
    return jnp.dot(a, b, preferred_element_type=F32)


def _dot_nt(a, b):
    return lax.dot_general(a, b, (((1,), (1,)), ((), ())), preferred_element_type=F32)


def _split_bf16(a):
    hi = a.astype(BF16)
    lo = (a - hi.astype(F32)).astype(BF16)
    return hi, lo


def _dot3(a, b):
    ah, al = _split_bf16(a)
    bh, bl = _split_bf16(b)
    return _dot(ah, bh) + (_dot(ah, bl) + _dot(al, bh))


def _dot3_nt(a, b):
    ah, al = _split_bf16(a)
    bh, bl = _split_bf16(b)
    return _dot_nt(ah, bh) + (_dot_nt(ah, bl) + _dot_nt(al, bh))


def _rms_rows(xf, g):
    ms = jnp.mean(xf * xf, axis=-1, keepdims=True)
    return xf * lax.rsqrt(ms + EPS) * g


def _sigmoid(z):
    return 1.0 / (1.0 + jnp.exp(-z))


PROLOGUE_BLOCKS = 2


def _slot_norm_rope(x, g_ref, seg_ref, inv_n, rope, shift):
    cos, sin_fwd, sin_bwd = rope
    slots = []
    for c in range(HW // MXU_DIM):
        xc = x[:, c * MXU_DIM:(c + 1) * MXU_DIM]
        ss = _dot((xc * xc).astype(BF16), seg_ref[...])
        xn = xc * lax.rsqrt(ss * inv_n + EPS) * g_ref[:, c * MXU_DIM:(c + 1) * MXU_DIM]
        for j in range(MXU_DIM // SLOT):
            xs = xn[:, j * SLOT:(j + 1) * SLOT]
            slots.append(xs * cos + pltpu.roll(xs, shift, 1) * sin_fwd
                         + pltpu.roll(xs, SLOT - shift, 1) * sin_bwd)
    return slots


def _prologue_kernel(x_ref, an_ref, w_ref, wuq_ref, wukv_ref, cqn_ref, ckvn_ref,
                     qna_ref, kna_ref, qnb_ref, knb_ref, seg_ref, place_ref,
                     ropea_ref, ropeb_ref,
                     qa_ref, ka_ref, va_ref, qb_ref, kb_ref, vb_ref, kmean_scr):
    @pl.when(pl.program_id(1) == 0)
    def _():
        kmean_scr[...] = jnp.zeros_like(kmean_scr)

    out_refs = (qa_ref, ka_ref, va_ref, qb_ref, kb_ref, vb_ref)
    for j in range(x_ref.shape[1] // MOBA_BLOCK):
        rows = slice(j * MOBA_BLOCK, (j + 1) * MOBA_BLOCK)
        hn = _rms_rows(x_ref[0, rows], an_ref[...]).astype(BF16)
        proj = _dot(hn, w_ref[...])
        _prologue_block(proj, pl.program_id(1) * (x_ref.shape[1] // MOBA_BLOCK) + j, rows,
                        wuq_ref, wukv_ref, cqn_ref, ckvn_ref, qna_ref, kna_ref, qnb_ref, knb_ref,
                        seg_ref, place_ref, [ropea_ref[t, rows] for t in range(3)],
                        [ropeb_ref[t, rows] for t in range(3)], out_refs, kmean_scr)


def _prologue_block(proj, s, rows, wuq_ref, wukv_ref, cqn_ref, ckvn_ref, qna_ref, kna_ref, qnb_ref, knb_ref,
                    seg_ref, place_ref, rope_a, rope_b, out_refs, kmean_scr):
    qa_ref, ka_ref, va_ref, qb_ref, kb_ref, vb_ref = out_refs
    tm = MOBA_BLOCK

    qa = _slot_norm_rope(proj[:, C_QA:C_QA + HW], qna_ref, seg_ref, 1.0 / A_HEAD_DIM, rope_a, A_ROT_DIM // 2)
    ka = _slot_norm_rope(proj[:, C_KA:C_KA + HW], kna_ref, seg_ref, 1.0 / A_HEAD_DIM, rope_a, A_ROT_DIM // 2)

    kmean = kmean_scr[...]
    gates = [_dot3_nt(kmean[:, h * SLOT:(h + 1) * SLOT], qa[h]) for h in range(N_HEADS)]
    g3 = jnp.stack(gates, axis=0)
    blk = lax.broadcasted_iota(jnp.int32, g3.shape, 1)
    g3 = jnp.where(blk < s, g3, -jnp.inf)
    rank3 = jnp.zeros(g3.shape, F32)
    for n in range(MAX_MOBA_BLOCKS):
        row = g3[:, n:n + 1, :]
        rank3 = rank3 + jnp.where((row > g3) | ((row == g3) & (blk > n)), 1.0, 0.0)
    sel3 = jnp.where(((rank3 < MOBA_TOPK) & (blk < s)) | (blk == s), 1.0, 0.0)
    sel_t = sel3.reshape(N_HEADS * MAX_MOBA_BLOCKS, tm)
    placed = _dot(sel_t.T.astype(BF16), place_ref[...])
    lane = lax.broadcasted_iota(jnp.int32, (tm, SLOT), 1)
    is_bias = (lane >= BIAS_LANE0) & (lane < BIAS_LANE0 + MAX_MOBA_BLOCKS)
    scale_a = LOG2E / math.sqrt(A_HEAD_DIM)
    for h in range(N_HEADS):
        sl = slice(h * SLOT, (h + 1) * SLOT)
        bias = (placed[:, sl] - 1.0) * (-NEG)
        qa_ref[0, rows, sl] = jnp.where(is_bias, bias, qa[h] * scale_a).astype(BF16)
        ka_ref[0, rows, sl] = jnp.where(lane == BIAS_LANE0 + s, 1.0, ka[h]).astype(BF16)
        va_ref[0, rows, sl] = jnp.where(lane == SUM_LANE[h % 2], 1.0,
                                        proj[:, C_VA + h * SLOT:C_VA + (h + 1) * SLOT]).astype(BF16)

    km = jnp.concatenate([jnp.mean(k, axis=0, keepdims=True) for k in ka], axis=1)
    kmean_scr[pl.ds(s, 1), :] = km

    cq = _rms_rows(proj[:, C_CQ:C_CQ + B_Q_RANK], cqn_ref[...]).astype(BF16)
    qb = _dot(cq, wuq_ref[...])
    ckv = _rms_rows(proj[:, C_CKV:C_CKV + B_KV_RANK], ckvn_ref[...]).astype(BF16)
    kv = _dot(ckv, wukv_ref[...])
    kr = pltpu.roll(proj[:, C_KR:C_KR + LANES], B_NOPE_DIM, 1)
    kb = kv[:, :HW] + jnp.concatenate([kr] * N_HEADS, axis=1)
    scale_b = LOG2E / math.sqrt(B_QK_DIM)
    qb_s = _slot_norm_rope(qb, qnb_ref, seg_ref, 1.0 / B_QK_DIM, rope_b, B_ROPE_DIM // 2)
    kb_s = _slot_norm_rope(kb, knb_ref, seg_ref, 1.0 / B_QK_DIM, rope_b, B_ROPE_DIM // 2)
    for h in range(N_HEADS):
        sl = slice(h * SLOT, (h + 1) * SLOT)
        qb_ref[0, rows, sl] = (qb_s[h] * scale_b).astype(BF16)
        kb_ref[0, rows, sl] = kb_s[h].astype(BF16)
        vb_ref[0, rows, sl] = jnp.where(lane == SUM_LANE[h % 2], 1.0,
                                        kv[:, HW + h * SLOT:HW + (h + 1) * SLOT]).astype(BF16)


def _prologue(x, p, rope_a, rope_b):
    b, s, _ = x.shape
    tm = PROLOGUE_BLOCKS * MOBA_BLOCK
    assert s % tm == 0 and s // MOBA_BLOCK <= MAX_MOBA_BLOCKS
    full = lambda a: pl.BlockSpec(a.shape, lambda i, j: (0,) * a.ndim)
    tok = lambda w: pl.BlockSpec((1, tm, w), lambda i, j: (i, j, 0))
    rope = pl.BlockSpec((3, tm, LANES), lambda i, j: (0, j, 0))
    consts = [p["attn_norm"], p["w_proj"], p["w_uq"], p["w_ukv"], p["cq_norm"], p["ckv_norm"],
              p["qn_a"], p["kn_a"], p["qn_b"], p["kn_b"], p["seg"], p["place"]]
    out_shape = [jax.ShapeDtypeStruct((b, s, HW), BF16)] * 6
    return pl.pallas_call(
        _prologue_kernel,
        grid=(b, s // tm),
        in_specs=[tok(D_MODEL)] + [full(a) for a in consts] + [rope, rope],
        out_specs=[tok(HW)] * 6,
        out_shape=out_shape,
        scratch_shapes=[pltpu.VMEM((MAX_MOBA_BLOCKS, HW), F32)],
        compiler_params=_cparams(2),
        name="prologue",
    )(x, *consts, rope_a, rope_b)


ATTN_TILE = 1024
ATTN_HEADS = 4
DIAG_SPLIT = 2


def _attn_kernel(q_ref, k_ref, v_ref, o_ref, m_scr, acc_scr):
    i = pl.program_id(2)
    t = q_ref.shape[1]
    tr = t // DIAG_SPLIT

    def scores(n, hh):
        sl = slice(hh * SLOT, (hh + 1) * SLOT)
        off = pl.multiple_of(n * t, t)
        return _dot_nt(q_ref[0, :, sl], k_ref[0, pl.ds(off, t), sl]), v_ref[0, pl.ds(off, t), sl]

    diag = pl.multiple_of(i * t, t)
    for hh in range(ATTN_HEADS):
        sl = slice(hh * SLOT, (hh + 1) * SLOT)
        for r in range(DIAG_SPLIT):
            rows = slice(r * tr, (r + 1) * tr)
            nk = (r + 1) * tr
            sc = _dot_nt(q_ref[0, rows, sl], k_ref[0, pl.ds(diag, nk), sl])
            row = lax.broadcasted_iota(jnp.int32, (tr, nk), 0)
            col = lax.broadcasted_iota(jnp.int32, (tr, nk), 1)
            sc = jnp.where(col <= row + r * tr, sc, NEG)
            m = jnp.max(sc, axis=1, keepdims=True)
            m_scr[hh, rows] = m
            acc_scr[hh, rows] = _dot(jnp.exp2(sc - m).astype(BF16), v_ref[0, pl.ds(diag, nk), sl])
    first = [(jnp.max(jnp.broadcast_to(m_scr[hh], (t, LANES)), axis=1, keepdims=True), acc_scr[hh])
             for hh in range(ATTN_HEADS)]

    def past_tile(n, state):
        new = []
        for hh, (m, acc) in enumerate(state):
            sc, v = scores(n, hh)
            m_new = jnp.maximum(m, jnp.max(sc, axis=1, keepdims=True))
            new.append((m_new, jnp.exp2(m - m_new) * acc + _dot(jnp.exp2(sc - m_new).astype(BF16), v)))
        return tuple(new)

    last = lax.fori_loop(0, i, past_tile, tuple(first))
    outs = [acc / acc[:, SUM_LANE[hh % 2]:SUM_LANE[hh % 2] + 1] for hh, (_, acc) in enumerate(last)]
    lane = lax.broadcasted_iota(jnp.int32, (t, LANES), 1)
    for p in range(ATTN_HEADS // 2):
        pair = jnp.where(lane < B_V_DIM, outs[2 * p], outs[2 * p + 1])
        o_ref[0, :, p * LANES:(p + 1) * LANES] = pair.astype(o_ref.dtype)


def _attention(q, k, v):
    b, s, _ = q.shape
    t = min(ATTN_TILE, s)
    assert s % t == 0 and t % MOBA_BLOCK == 0
    return pl.pallas_call(
        _attn_kernel,
        grid=(b, N_HEADS // ATTN_HEADS, s // t),
        in_specs=[pl.BlockSpec((1, t, ATTN_HEADS * SLOT), lambda bi, p, i: (bi, i, p)),
                  pl.BlockSpec((1, s, ATTN_HEADS * SLOT), lambda bi, p, i: (bi, 0, p)),
                  pl.BlockSpec((1, s, ATTN_HEADS * SLOT), lambda bi, p, i: (bi, 0, p))],
        out_specs=pl.BlockSpec((1, t, ATTN_HEADS * B_V_DIM), lambda bi, p, i: (bi, i, p)),
        out_shape=jax.ShapeDtypeStruct((b, s, V_WIDTH), BF16),
        scratch_shapes=[pltpu.VMEM((ATTN_HEADS, t, 1), F32), pltpu.VMEM((ATTN_HEADS, t, LANES), F32)],
        compiler_params=_cparams(3),
        name="attention",
    )(q, k, v)


def _merge_rows(x_ref, an_ref, ya_ref, yb_ref, wg_ref, wa_ref, wb_ref, wo_ref):
    xf = x_ref[...]
    hn = _rms_rows(xf, an_ref[...]).astype(BF16)
    gates = _dot(hn, wg_ref[...])
    pa = _dot(ya_ref[...], wa_ref[...])
    pb = _dot(yb_ref[...], wb_ref[...])
    merged = _sigmoid(gates[:, :D_MODEL]) * pa + _sigmoid(gates[:, D_MODEL:]) * pb
    return xf + _dot(merged.astype(BF16), wo_ref[...])


def _merge_kernel(x_ref, an_ref, ya_ref, yb_ref, wg_ref, wa_ref, wb_ref, wo_ref, o_ref):
    o_ref[...] = _merge_rows(x_ref, an_ref, ya_ref, yb_ref, wg_ref, wa_ref, wb_ref, wo_ref)


def _merge(x2, p, ya2, yb2, tm=512):
    t = x2.shape[0]
    full = lambda a: pl.BlockSpec(a.shape, lambda i: (0,) * a.ndim)
    tok = lambda w: pl.BlockSpec((tm, w), lambda i: (i, 0))
    consts = [p["w_gates"], p["w_br_a"], p["w_br_b"], p["w_out"]]
    return pl.pallas_call(
        _merge_kernel,
        grid=(t // tm,),
        in_specs=[tok(D_MODEL), full(p["attn_norm"]), tok(V_WIDTH), tok(V_WIDTH)] + [full(a) for a in consts],
        out_specs=tok(D_MODEL),
        out_shape=jax.ShapeDtypeStruct((t, D_MODEL), F32),
        compiler_params=_cparams(1),
        name="merge",
    )(x2, p["attn_norm"], ya2, yb2, *consts)


FF_CHUNK = MXU_DIM


def _ffn_rows(xf, fn_ref, wgu_ref, wd_ref, act_scr):
    h = _rms_rows(xf, fn_ref[...]).astype(BF16)
    for c in range(D_FF // FF_CHUNK):
        g = _dot(h, wgu_ref[:, c * FF_CHUNK:(c + 1) * FF_CHUNK])
        u = _dot(h, wgu_ref[:, D_FF + c * FF_CHUNK:D_FF + (c + 1) * FF_CHUNK])
        act_scr[:, c * FF_CHUNK:(c + 1) * FF_CHUNK] = (g * _sigmoid(g) * u).astype(BF16)
    return xf + _dot(act_scr[...], wd_ref[...])


def _merge_ffn_kernel(x_ref, an_ref, ya_ref, yb_ref, wg_ref, wa_ref, wb_ref, wo_ref, fn_ref, wgu_ref, wd_ref,
                      o_ref, act_scr):
    x1 = _merge_rows(x_ref, an_ref, ya_ref, yb_ref, wg_ref, wa_ref, wb_ref, wo_ref)
    o_ref[...] = _ffn_rows(x1, fn_ref, wgu_ref, wd_ref, act_scr)


def _merge_ffn_dense(x2, p, ya2, yb2, fn, w_gu, w_down, tm=512):
    t = x2.shape[0]
    full = lambda a: pl.BlockSpec(a.shape, lambda i: (0,) * a.ndim)
    tok = lambda w: pl.BlockSpec((tm, w), lambda i: (i, 0))
    consts = [p["w_gates"], p["w_br_a"], p["w_br_b"], p["w_out"], fn, w_gu, w_down]
    return pl.pallas_call(
        _merge_ffn_kernel,
        grid=(t // tm,),
        in_specs=[tok(D_MODEL), full(p["attn_norm"]), tok(V_WIDTH), tok(V_WIDTH)] + [full(a) for a in consts],
        out_specs=tok(D_MODEL),
        out_shape=jax.ShapeDtypeStruct((t, D_MODEL), F32),
        scratch_shapes=[pltpu.VMEM((tm, D_FF), BF16)],
        compiler_params=_cparams(1),
        name="merge_ffn_dense",
    )(x2, p["attn_norm"], ya2, yb2, *consts)


MOE_TILE = 512
MOE_CHUNK = 512
RUN_ALIGN = 16
MOE_ROWS = -(-(2 * MOE_CHUNK + N_EXPERTS * RUN_ALIGN) // LANES) * LANES
RUN_BITS = range(RUN_ALIGN.bit_length() - 1, (2 * MOE_CHUNK).bit_length())
R_POS1, R_POS2, R_P1, R_P2 = range(4)
S_CNT, S_LOCAL, S_GLOBAL = range(3)


def _run_rows(first, size):
    return pl.ds(pl.multiple_of(first, RUN_ALIGN), size)


def _run_copies(table, src_of, dst_of, sem, start):
    for e in range(N_EXPERTS):
        n = table[S_CNT, e]
        for bit in RUN_BITS:
            size = 1 << bit
            above = n & ~(2 * size - 1)

            @pl.when((n & size) != 0)
            def _(e=e, size=size, above=above):
                cp = pltpu.make_async_copy(src_of(e, above, size), dst_of(e, above, size), sem)
                if start:
                    cp.start(priority=e % 2)
                else:
                    cp.wait()


def _router_kernel(x_ref, fn_ref, rw_ref, route_ref, cnt_ref, xs_ref,
                   carry_scr, stage, tab_v, tab_s, zeros_v, sem, tab_sem):
    c = pl.program_id(0)
    last = pl.num_programs(0) - 1
    tm = x_ref.shape[0]
    slot = c % 2
    cap = (xs_ref.shape[0] - MOE_TILE) // N_EXPERTS

    @pl.when(c == 0)
    def _():
        carry_scr[...] = jnp.zeros_like(carry_scr)

    hf = _rms_rows(x_ref[...], fn_ref[...])
    lane = lax.broadcasted_iota(jnp.int32, (tm, LANES), 1)
    logits = jnp.where(lane < N_EXPERTS, _dot3(hf, rw_ref[...]), -jnp.inf)
    m1 = jnp.max(logits, axis=1, keepdims=True)
    i1 = jnp.min(jnp.where(logits == m1, lane, LANES), axis=1, keepdims=True)
    rest = jnp.where(lane == i1, -jnp.inf, logits)
    m2 = jnp.max(rest, axis=1, keepdims=True)
    i2 = jnp.min(jnp.where(rest == m2, lane, LANES), axis=1, keepdims=True)
    e2 = jnp.exp(m2 - m1)
    p1 = 1.0 / (1.0 + e2)
    p2 = e2 / (1.0 + e2)

    onehot = jnp.where((lane == i1) | (lane == i2), 1.0, 0.0)
    r = lax.broadcasted_iota(jnp.int32, (tm, tm), 0)
    cc = lax.broadcasted_iota(jnp.int32, (tm, tm), 1)
    earlier = jnp.where(cc < r, 1.0, 0.0).astype(BF16)
    before = _dot(earlier, onehot.astype(BF16))
    er = lax.broadcasted_iota(jnp.int32, (LANES, LANES), 0)
    ec = lax.broadcasted_iota(jnp.int32, (LANES, LANES), 1)
    cnt = jnp.floor((jnp.sum(onehot, axis=0, keepdims=True) + (RUN_ALIGN - 1)) * (1.0 / RUN_ALIGN)) * RUN_ALIGN
    local = _dot(jnp.broadcast_to(cnt, (8, LANES)).astype(BF16), jnp.where(er < ec, 1.0, 0.0).astype(BF16))[0:1]
    pos = before + local
    pos1 = jnp.sum(jnp.where(lane == i1, pos, 0.0), axis=1, keepdims=True)
    pos2 = jnp.sum(jnp.where(lane == i2, pos, 0.0), axis=1, keepdims=True)
    rec = jnp.zeros((tm, LANES), F32)
    for where_, val in ((R_POS1, pos1), (R_POS2, pos2), (R_P1, p1), (R_P2, p2)):
        rec = jnp.where(lane == where_, val, rec)
    route_ref[...] = rec

    slot_lane = lax.broadcasted_iota(jnp.int32, (tm, MOE_ROWS), 1).astype(F32)
    pick_t = jnp.where((slot_lane == pos1) | (slot_lane == pos2), 1.0, 0.0)
    sorted_rows = _dot(pick_t.T.astype(BF16), hf.astype(BF16))

    stage[slot] = sorted_rows.astype(BF16)

    carry = carry_scr[0:1, :]
    srow = lax.broadcasted_iota(jnp.int32, (8, LANES), 0)
    elane = lax.broadcasted_iota(jnp.int32, (8, LANES), 1)
    table = jnp.where(srow == S_CNT, cnt, jnp.where(srow == S_LOCAL, local, carry + (elane * cap).astype(F32)))
    tab_v[...] = table.astype(jnp.int32)
    total = carry + cnt
    carry_scr[...] = jnp.broadcast_to(total, carry_scr.shape)
    cnt_ref[0] = jnp.broadcast_to(cnt, (8, LANES))

    tcp = pltpu.make_async_copy(tab_v, tab_s.at[slot], tab_sem)
    tcp.start()
    tcp.wait()

    def runs(s, start):
        _run_copies(tab_s.at[s],
                    lambda e, off, size: stage.at[s, _run_rows(tab_s[s, S_LOCAL, e] + off, size)],
                    lambda e, off, size: xs_ref.at[_run_rows(tab_s[s, S_GLOBAL, e] + off, size)],
                    sem.at[s], start)

    runs(slot, True)

    @pl.when(c > 0)
    def _():
        runs(1 - slot, False)

    @pl.when(c == last)
    def _():
        runs(slot, False)
        zeros_v[...] = jnp.zeros_like(zeros_v)
        pad = (-total.astype(jnp.int32)) & (MOE_TILE - 1)
        tab_v[...] = jnp.where(srow == S_CNT, pad, jnp.where(srow == S_LOCAL, 0,
                                                               total.astype(jnp.int32) + elane * cap))
        pcp = pltpu.make_async_copy(tab_v, tab_s.at[slot], tab_sem)
        pcp.start()
        pcp.wait()
        for start in (True, False):
            _run_copies(tab_s.at[slot],
                        lambda e, off, size: zeros_v.at[_run_rows(off, size)],
                        lambda e, off, size: xs_ref.at[_run_rows(tab_s[slot, S_GLOBAL, e] + off, size)],
                        sem.at[slot], start)


def _expert_region(t):
    worst = t + (t // MOE_CHUNK) * (RUN_ALIGN - 1)
    return -(-worst // MOE_TILE) * MOE_TILE


def _router(x2, fn, router_w):
    t = x2.shape[0]
    tm = MOE_CHUNK
    nc = t // tm
    n_rows = N_EXPERTS * _expert_region(t) + MOE_TILE
    full = lambda a: pl.BlockSpec(a.shape, lambda i: (0,) * a.ndim)
    return pl.pallas_call(
        _router_kernel,
        grid=(nc,),
        in_specs=[pl.BlockSpec((tm, D_MODEL), lambda i: (i, 0)), full(fn), full(router_w)],
        out_specs=[pl.BlockSpec((tm, LANES), lambda i: (i, 0)), pl.BlockSpec((1, 8, LANES), lambda i: (i, 0, 0)),
                   pl.BlockSpec(memory_space=pl.ANY)],
        out_shape=[jax.ShapeDtypeStruct((t, LANES), F32), jax.ShapeDtypeStruct((nc, 8, LANES), F32),
                   jax.ShapeDtypeStruct((n_rows, D_MODEL), BF16)],
        scratch_shapes=[pltpu.VMEM((8, LANES), F32), pltpu.VMEM((2, MOE_ROWS, D_MODEL), BF16),
                        pltpu.VMEM((8, LANES), jnp.int32), pltpu.SMEM((2, 8, LANES), jnp.int32),
                        pltpu.VMEM((MOE_TILE, D_MODEL), BF16),
                        pltpu.SemaphoreType.DMA((2,)), pltpu.SemaphoreType.DMA(())],
        compiler_params=_cparams(1),
        name="moe_router",
    )(x2, fn, router_w)


def _experts_kernel(tb_ref, te_ref, tv_ref, xs_ref, wgu_ref, wd_ref, y_ref, act_scr):
    del tb_ref, te_ref
    j = pl.program_id(0)

    @pl.when(tv_ref[j] != 0)
    def _():
        h = xs_ref[...]
        for c in range(D_FF // FF_CHUNK):
            g = _dot(h, wgu_ref[0, :, c * FF_CHUNK:(c + 1) * FF_CHUNK])
            u = _dot(h, wgu_ref[0, :, D_FF + c * FF_CHUNK:D_FF + (c + 1) * FF_CHUNK])
            act_scr[:, c * FF_CHUNK:(c + 1) * FF_CHUNK] = (g * _sigmoid(g) * u).astype(BF16)
        y_ref[...] = _dot(act_scr[...], wd_ref[0]).astype(BF16)

    @pl.when(tv_ref[j] == 0)
    def _():
        y_ref[...] = jnp.zeros_like(y_ref)


def _experts(tile_block, tile_expert, tile_valid, xs, w_gu, w_down):
    tm = MOE_TILE
    row = pl.BlockSpec((tm, D_MODEL), lambda j, tb, te, tv: (tb[j], 0))
    return pl.pallas_call(
        _experts_kernel,
        grid_spec=pltpu.PrefetchScalarGridSpec(
            num_scalar_prefetch=3,
            grid=(tile_block.shape[0],),
            in_specs=[row,
                      pl.BlockSpec((1, D_MODEL, 2 * D_FF), lambda j, tb, te, tv: (te[j], 0, 0)),
                      pl.BlockSpec((1, D_FF, D_MODEL), lambda j, tb, te, tv: (te[j], 0, 0))],
            out_specs=row,
            scratch_shapes=[pltpu.VMEM((tm, D_FF), BF16)]),
        out_shape=jax.ShapeDtypeStruct(xs.shape, BF16),
        compiler_params=_cparams(1),
        name="moe_experts",
    )(tile_block, tile_expert, tile_valid, xs, w_gu, w_down)


def _combine_kernel(tcur_ref, tnext_ref, x_ref, route_ref, y_ref, o_ref, ybuf, sem):
    c = pl.program_id(0)
    last = pl.num_programs(0) - 1
    tm = x_ref.shape[0]
    slot = c % 2

    def runs(table, s, start):
        _run_copies(table.at[0],
                    lambda e, off, size: y_ref.at[_run_rows(table[0, S_GLOBAL, e] + off, size)],
                    lambda e, off, size: ybuf.at[s, _run_rows(table[0, S_LOCAL, e] + off, size)],
                    sem.at[s], start)

    @pl.when(c == 0)
    def _():
        ybuf[...] = jnp.zeros_like(ybuf)
        runs(tcur_ref, 0, True)

    @pl.when(c < last)
    def _():
        runs(tnext_ref, 1 - slot, True)

    runs(tcur_ref, slot, False)

    rec = route_ref[...]
    ys = ybuf[slot]
    slot_lane = lax.broadcasted_iota(jnp.int32, (tm, MOE_ROWS), 1).astype(F32)
    out = x_ref[...]
    for pos_lane, p_lane in ((R_POS1, R_P1), (R_POS2, R_P2)):
        pick = jnp.where(slot_lane == rec[:, pos_lane:pos_lane + 1], 1.0, 0.0).astype(BF16)
        out = out + rec[:, p_lane:p_lane + 1] * _dot(pick, ys)
    o_ref[...] = out


def _combine(tables, x2, route, y):
    t = x2.shape[0]
    tm = MOE_CHUNK
    nc = t // tm
    tok = lambda w: pl.BlockSpec((tm, w), lambda c: (c, 0))
    tab = lambda f: pl.BlockSpec((1, 3, N_EXPERTS), lambda c: (f(c), 0, 0), memory_space=pltpu.SMEM)
    return pl.pallas_call(
        _combine_kernel,
        grid=(nc,),
        in_specs=[tab(lambda c: c), tab(lambda c: jnp.minimum(c + 1, nc - 1)),
                  tok(D_MODEL), tok(LANES), pl.BlockSpec(memory_space=pl.ANY)],
        out_specs=tok(D_MODEL),
        out_shape=jax.ShapeDtypeStruct((t, D_MODEL), F32),
        scratch_shapes=[pltpu.VMEM((2, MOE_ROWS, D_MODEL), BF16), pltpu.SemaphoreType.DMA((2,))],
        compiler_params=_cparams(1),
        name="moe_combine",
    )(tables, tables, x2, route, y)


def _ffn_moe(x2, fn, router_w, w_gu, w_down):
    t = x2.shape[0]
    route, cnt, xs = _router(x2, fn, router_w)
    counts = cnt[:, 0, :N_EXPERTS].astype(jnp.int32)
    upto = jnp.cumsum(counts, axis=0)
    experts = jnp.arange(N_EXPERTS, dtype=jnp.int32)
    region = _expert_region(t)
    tables = jnp.stack([counts, jnp.cumsum(counts, axis=1) - counts,
                        upto - counts + experts[None, :] * region], axis=1)
    tiles = (upto[-1] + MOE_TILE - 1) // MOE_TILE
    tile_end = jnp.cumsum(tiles)
    max_rows = 2 * t + (t // MOE_CHUNK) * N_EXPERTS * (RUN_ALIGN - 1)
    n_steps = max_rows // MOE_TILE + N_EXPERTS
    j = jnp.arange(n_steps, dtype=jnp.int32)
    tile_expert = jnp.minimum(jnp.sum((j[:, None] >= tile_end[None, :]).astype(jnp.int32), axis=1), N_EXPERTS - 1)
    first_tile = jnp.sum(jnp.where(experts[None, :] == tile_expert[:, None], (tile_end - tiles)[None, :], 0), axis=1)
    tile_valid = (j < tile_end[-1]).astype(jnp.int32)
    spare = N_EXPERTS * region // MOE_TILE
    tile_block = jnp.where(tile_valid != 0, tile_expert * (region // MOE_TILE) + (j - first_tile), spare)
    y = _experts(tile_block, tile_expert, tile_valid, xs, w_gu, w_down)
    return _combine(tables, x2, route, y)


def _slot_cols(w, dims):
    k = w.shape[0]
    w = w.reshape(k, N_HEADS, dims)
    return jnp.pad(w, ((0, 0), (0, 0), (0, SLOT - dims))).reshape(k, HW)


def _slot_cols_v(w):
    k = w.shape[0]
    w = w.reshape(k, N_HEADS // 2, 2, B_V_DIM)
    zero = jnp.zeros((k, N_HEADS // 2, B_V_DIM), w.dtype)
    even = jnp.concatenate([w[:, :, 0], zero], axis=-1)
    odd = jnp.concatenate([zero, w[:, :, 1]], axis=-1)
    return jnp.stack([even, odd], axis=2).reshape(k, HW)


def _slot_vec(g, dims):
    return jnp.tile(jnp.pad(g, (0, SLOT - dims)), N_HEADS).reshape(1, HW)


def _rope_table(seq, rot_dim, lane0):
    half = rot_dim // 2
    lane = np.arange(SLOT)
    rot = (lane >= lane0) & (lane < lane0 + rot_dim)
    pair = np.where(rot, (lane - lane0) % half, 0)
    inv = ROPE_THETA ** (-jnp.arange(0, rot_dim, 2, dtype=F32) / rot_dim)
    inv = jnp.where(rot, inv[pair], 0.0)
    ang = jnp.arange(seq, dtype=F32)[:, None] * inv[None, :]
    upper = (rot & (lane - lane0 >= half)).astype(np.float32)
    lower = (rot & (lane - lane0 < half)).astype(np.float32)
    sin = jnp.sin(ang)
    return jnp.stack([jnp.cos(ang), sin * upper, -sin * lower])


def _constants():
    r = np.arange(MXU_DIM)
    seg = (r[:, None] // SLOT == r[None, :] // SLOT).astype(np.float32)
    place = np.zeros((N_HEADS * MAX_MOBA_BLOCKS, HW), np.float32)
    for h in range(N_HEADS):
        for n in range(MAX_MOBA_BLOCKS):
            place[h * MAX_MOBA_BLOCKS + n, h * SLOT + BIAS_LANE0 + n] = 1.0
    return jnp.asarray(seg, BF16), jnp.asarray(place, BF16)


def _layer_params(l, attn_norm, w_in, moba_q_norm, moba_k_norm, mla_cq_norm, w_uq, mla_ckv_norm,
                  w_ukv, mla_q_norm, mla_k_norm, w_branch_a, w_branch_b, w_out):
    wi = w_in[l]
    aw = N_HEADS * A_HEAD_DIM
    o_cq = 3 * aw
    o_ckv = o_cq + B_Q_RANK
    o_kr = o_ckv + B_KV_RANK
    o_g = o_kr + B_ROPE_DIM
    w_proj = jnp.concatenate([
        _slot_cols(wi[:, 0:aw], A_HEAD_DIM), _slot_cols(wi[:, aw:2 * aw], A_HEAD_DIM),
        _slot_cols_v(wi[:, 2 * aw:3 * aw]),
        wi[:, o_cq:o_ckv], wi[:, o_ckv:o_kr], jnp.pad(wi[:, o_kr:o_g], ((0, 0), (0, LANES - B_ROPE_DIM)))], axis=1)
    ukv = w_ukv[l].reshape(B_KV_RANK, N_HEADS, B_NOPE_DIM + B_V_DIM)
    w_ukv_s = jnp.concatenate([_slot_cols(ukv[:, :, :B_NOPE_DIM].reshape(B_KV_RANK, -1), B_NOPE_DIM),
                               _slot_cols_v(ukv[:, :, B_NOPE_DIM:].reshape(B_KV_RANK, -1))], axis=1)
    seg, place = _constants()
    return {
        "attn_norm": attn_norm[l].reshape(1, D_MODEL),
        "w_proj": w_proj.astype(BF16),
        "w_gates": wi[:, o_g:].astype(BF16),
        "w_uq": _slot_cols(w_uq[l], B_QK_DIM).astype(BF16),
        "w_ukv": w_ukv_s.astype(BF16),
        "cq_norm": mla_cq_norm[l].reshape(1, B_Q_RANK),
        "ckv_norm": mla_ckv_norm[l].reshape(1, B_KV_RANK),
        "qn_a": _slot_vec(moba_q_norm[l], A_HEAD_DIM),
        "kn_a": _slot_vec(moba_k_norm[l], A_HEAD_DIM),
        "qn_b": _slot_vec(mla_q_norm[l], B_QK_DIM),
        "kn_b": _slot_vec(mla_k_norm[l], B_QK_DIM),
        "seg": seg,
        "place": place,
        "w_br_a": w_branch_a[l].astype(BF16),
        "w_br_b": w_branch_b[l].astype(BF16),
        "w_out": w_out[l].astype(BF16),
    }


def kernel(x, attn_norm, w_in, moba_q_norm, moba_k_norm, mla_cq_norm, w_uq, mla_ckv_norm, w_ukv,
           mla_q_norm, mla_k_norm, w_branch_a, w_branch_b, w_out, ffn_norm, dense_w_gate_up,
           dense_w_down, router_w, expert_w_gate_up, expert_w_down):
    b, s, d = x.shape
    depth = attn_norm.shape[0]
    rope_a = _rope_table(s, A_ROT_DIM, 0)
    rope_b = _rope_table(s, B_ROPE_DIM, B_NOPE_DIM)
    for l in range(depth):
        p = _layer_params(l, attn_norm, w_in, moba_q_norm, moba_k_norm, mla_cq_norm, w_uq,
                          mla_ckv_norm, w_ukv, mla_q_norm, mla_k_norm, w_branch_a, w_branch_b, w_out)
        qa, ka, va, qb, kb, vb = _prologue(x, p, rope_a, rope_b)
        ya = _attention(qa, ka, va)
        yb = _attention(qb, kb, vb)
        mixer_in = (x.reshape(b * s, d), p, ya.reshape(b * s, V_WIDTH), yb.reshape(b * s, V_WIDTH))
        fn = ffn_norm[l].reshape(1, D_MODEL)
        if l % 2 == 0:
            x2 = _merge_ffn_dense(*mixer_in, fn, dense_w_gate_up[l // 2].astype(BF16),
                                  dense_w_down[l // 2].astype(BF16))
        else:
            rw = jnp.pad(router_w[l // 2], ((0, 0), (0, LANES - N_EXPERTS)))
            x2 = _ffn_moe(_merge(*mixer_in), fn, rw, expert_w_gate_up[l // 2].astype(BF16),
                          expert_w_down[l // 2].astype(BF16))
        x = x2.reshape(b, s, d)
    return x
```
